```python
import math
import jax, jax.numpy as jnp
from jax import lax
import numpy as np

D_MODEL = 2048
BATCH = 4
SEQ = 4096
DEPTH = 2

GROUP_WIDTH = D_MODEL // 4
MIX_WIDTH = 4 * GROUP_WIDTH
D_FF = 4 * D_MODEL
S5_CH = 16
S5_GROUPS = GROUP_WIDTH // S5_CH
S5_STATE = 64
S5_DT_MIN = 1e-3
S5_DT_MAX = 1e-1
RET_HEADS = 4
RET_QK = 64
RET_V = GROUP_WIDTH // RET_HEADS
RET_CHUNK = 128
SWA_HD = 64
SWA_HEADS = GROUP_WIDTH // SWA_HD
SWA_KV_HEADS = 2
WINDOW = 128
MLA_HEADS = 4
MLA_Q_RANK = 384
MLA_KV_RANK = 128
MLA_NOPE = 128
MLA_ROPE = 64
MLA_V = GROUP_WIDTH // MLA_HEADS
MLA_BLOCK = 128
ROPE_BASE = 10000.0
EPS = 1e-6
NEG = -1e30

IN_SPLITS = (
    GROUP_WIDTH,
    RET_HEADS * RET_QK,
    RET_HEADS * RET_QK,
    GROUP_WIDTH,
    GROUP_WIDTH,
    SWA_HEADS * SWA_HD,
    SWA_KV_HEADS * SWA_HD,
    SWA_KV_HEADS * SWA_HD,
    MLA_Q_RANK,
    MLA_KV_RANK,
    MLA_ROPE,
)
N_IN = sum(IN_SPLITS)
SPLIT_IDX = [int(v) for v in np.cumsum(IN_SPLITS)[:-1]]

kernel_name = 'hybrid_parallel_head_group_block'


def rms_norm(x, gain=None):
    xf = x.astype(jnp.float32)
    y = xf * lax.rsqrt(jnp.mean(xf * xf, axis=-1, keepdims=True) + EPS)
    if gain is not None:
        y = y * gain.astype(jnp.float32)
    return y.astype(x.dtype)


def rotary(x):
    seq, d = x.shape[1], x.shape[-1]
    inv = ROPE_BASE ** (-jnp.arange(0, d, 2, dtype=jnp.float32) / d)
    ang = jnp.arange(seq, dtype=jnp.float32)[:, None] * inv[None, :]
    cos = jnp.cos(ang)[None, :, None, :].astype(x.dtype)
    sin = jnp.sin(ang)[None, :, None, :].astype(x.dtype)
    x1, x2 = x[..., :d // 2], x[..., d // 2:]
    return jnp.concatenate([x1 * cos - x2 * sin, x1 * sin + x2 * cos], axis=-1)


def s5_mixer(u, lam_re, lam_im, log_dt, b_re, b_im, c_re, c_im, d_skip, glu_w, glu_b):
    f32 = jnp.float32
    bsz, seq, _ = u.shape
    uf = u.astype(f32).reshape(bsz, seq, S5_GROUPS, S5_CH)
    dt = jnp.exp(log_dt.astype(f32))[:, None]
    lr, li = lam_re.astype(f32), lam_im.astype(f32)
    mag = jnp.exp(lr * dt)
    ar, ai = mag * jnp.cos(li * dt), mag * jnp.sin(li * dt)
    den = lr * lr + li * li
    cr = ((ar - 1.0) * lr + ai * li) / den
    ci = (ai * lr - (ar - 1.0) * li) / den
    br, bi = b_re.astype(f32), b_im.astype(f32)
    bbar_r = cr[..., None] * br - ci[..., None] * bi
    bbar_i = cr[..., None] * bi + ci[..., None] * br
    bu_r = jnp.einsum('bsgh,gph->bsgp', uf, bbar_r)
    bu_i = jnp.einsum('bsgh,gph->bsgp', uf, bbar_i)
    a_r = jnp.broadcast_to(ar, bu_r.shape)
    a_i = jnp.broadcast_to(ai, bu_i.shape)

    def combine(e1, e2):
        a1r, a1i, b1r, b1i = e1
        a2r, a2i, b2r, b2i = e2
        return (a2r * a1r - a2i * a1i, a2r * a1i + a2i * a1r,
                a2r * b1r - a2i * b1i + b2r, a2r * b1i + a2i * b1r + b2i)

    _, _, st_r, st_i = lax.associative_scan(combine, (a_r, a_i, bu_r, bu_i), axis=1)
    y = (jnp.einsum('bsgp,ghp->bsgh', st_r, c_re.astype(f32))
         - jnp.einsum('bsgp,ghp->bsgh', st_i, c_im.astype(f32))
         + d_skip.astype(f32) * uf)
    z = jax.nn.gelu(y.reshape(bsz, seq, GROUP_WIDTH)).astype(u.dtype)
    return z * jax.nn.sigmoid(z @ glu_w + glu_b)


def retention_mixer(q, k, v, g):
    f32 = jnp.float32
    bsz, seq = q.shape[:2]
    nck = seq // RET_CHUNK
    q = rotary(q.astype(f32))
    k = rotary(k.astype(f32)) * (RET_QK ** -0.5)
    v = v.astype(f32)
    log_gamma = jnp.log1p(-(2.0 ** (-5.0 - jnp.arange(RET_HEADS, dtype=f32))))
    idx = jnp.arange(RET_CHUNK, dtype=f32)
    rel = idx[:, None] - idx[None, :]
    decay_intra = jnp.where(rel >= 0, jnp.exp(log_gamma[:, None, None] * jnp.maximum(rel, 0.0)), 0.0)
    zeta = jnp.exp(log_gamma[:, None] * (RET_CHUNK - 1.0 - idx))
    xi = jnp.exp(log_gamma[:, None] * (idx + 1.0))
    gamma_chunk = jnp.exp(log_gamma * RET_CHUNK)
    qc = q.reshape(bsz, nck, RET_CHUNK, RET_HEADS, RET_QK)
    kc = k.reshape(bsz, nck, RET_CHUNK, RET_HEADS, RET_QK)
    vc = v.reshape(bsz, nck, RET_CHUNK, RET_HEADS, RET_V)
    s = jnp.einsum('bnchd,bnmhd->bnhcm', qc, kc) * decay_intra
    o_intra = jnp.einsum('bnhcm,bnmhv->bnchv', s, vc)
    kv = jnp.einsum('bnmhd,hm,bnmhv->nbhdv', kc, zeta, vc)

    def step(state, kv_n):
        return gamma_chunk[None, :, None, None] * state + kv_n, state

    _, prev = lax.scan(step, jnp.zeros_like(kv[0]), kv)
    o_cross = jnp.einsum('bnchd,nbhdv->bnchv', qc, prev) * xi.T[None, None, :, :, None]
    o = (o_intra + o_cross).reshape(bsz, seq, RET_HEADS, RET_V)
    o = o * lax.rsqrt(jnp.mean(o * o, axis=-1, keepdims=True) + EPS)
    return o.reshape(bsz, seq, GROUP_WIDTH).astype(g.dtype) * jax.nn.silu(g)


def swa_mixer(q, k, v, sinks):
    f32 = jnp.float32
    bsz, seq = q.shape[:2]
    nb = seq // WINDOW
    grp = SWA_HEADS // SWA_KV_HEADS
    qb = q.reshape(bsz, nb, WINDOW, SWA_KV_HEADS, grp, SWA_HD)
    pad = ((0, 0), (WINDOW, 0), (0, 0), (0, 0))
    kp = jnp.pad(k, pad).reshape(bsz, nb + 1, WINDOW, SWA_KV_HEADS, SWA_HD)
    vp = jnp.pad(v, pad).reshape(bsz, nb + 1, WINDOW, SWA_KV_HEADS, SWA_HD)
    kb = jnp.concatenate([kp[:, :-1], kp[:, 1:]], axis=2)
    vb = jnp.concatenate([vp[:, :-1], vp[:, 1:]], axis=2)
    s = jnp.einsum('bnqkgd,bnjkd->bnkgqj', qb, kb).astype(f32) * (SWA_HD ** -0.5)
    r = jnp.arange(WINDOW)[:, None]
    j = jnp.arange(2 * WINDOW)[None, :]
    dist = r + WINDOW - j
    blk = jnp.arange(nb)[:, None, None]
    valid = (dist >= 0) & (dist < WINDOW) & (blk * WINDOW + j - WINDOW >= 0)
    s = jnp.where(valid[None, :, None, None], s, NEG)
    sink = sinks.astype(f32).reshape(SWA_KV_HEADS, grp)[None, None, :, :, None, None]
    m = jnp.maximum(jnp.max(s, axis=-1, keepdims=True), sink)
    p = jnp.exp(s - m)
    denom = jnp.sum(p, axis=-1, keepdims=True) + jnp.exp(sink - m)
    o = jnp.einsum('bnkgqj,bnjkd->bnqkgd', (p / denom).astype(v.dtype), vb)
    return o.reshape(bsz, seq, GROUP_WIDTH)


def mla_mixer(c_q, c_kv, k_rope, q_norm, kv_norm, w_uq, w_ukv):
    f32 = jnp.float32
    bsz, seq = c_q.shape[:2]
    q = (rms_norm(c_q, q_norm) @ w_uq).reshape(bsz, seq, MLA_HEADS, MLA_NOPE + MLA_ROPE)
    q_nope, q_rope = q[..., :MLA_NOPE], rotary(q[..., MLA_NOPE:])
    kv = (rms_norm(c_kv, kv_norm) @ w_ukv).reshape(bsz, seq, MLA_HEADS, MLA_NOPE + MLA_V)
    k_nope, v = kv[..., :MLA_NOPE], kv[..., MLA_NOPE:]
    k_r = rotary(k_rope[:, :, None, :])[:, :, 0]
    scale = (MLA_NOPE + MLA_ROPE) ** -0.5
    nb = seq // MLA_BLOCK
    k_pos = jnp.arange(seq)

    def block(args):
        i, qn, qr = args
        s = (jnp.einsum('bqhd,bkhd->bhqk', qn, k_nope)
             + jnp.einsum('bqhr,bkr->bhqk', qr, k_r)).astype(f32) * scale
        q_pos = i * MLA_BLOCK + jnp.arange(MLA_BLOCK)
        s = jnp.where(k_pos[None, :] <= q_pos[:, None], s, NEG)
        p = jax.nn.softmax(s, axis=-1).astype(v.dtype)
        return jnp.einsum('bhqk,bkhv->bqhv', p, v)

    qn_b = q_nope.reshape(bsz, nb, MLA_BLOCK, MLA_HEADS, MLA_NOPE).transpose(1, 0, 2, 3, 4)
    qr_b = q_rope.reshape(bsz, nb, MLA_BLOCK, MLA_HEADS, MLA_ROPE).transpose(1, 0, 2, 3, 4)
    o = lax.map(block, (jnp.arange(nb), qn_b, qr_b))
    return o.transpose(1, 0, 2, 3, 4).reshape(bsz, seq, GROUP_WIDTH)


def setup_inputs(seed: int = 0) -> dict:
    key = jax.random.key(seed)
    ks = jax.random.split(key, 32)
    f32 = jnp.float32
    nrm = lambda k, shape, scale: scale * jax.random.normal(k, shape, f32)
    L, D, G, P, H, GW = DEPTH, D_MODEL, S5_GROUPS, S5_STATE, S5_CH, GROUP_WIDTH
    return {
        'x': nrm(ks[0], (BATCH, SEQ, D), 1.0),
        'c': nrm(ks[1], (BATCH, D), 1.0),
        'norm1_g': 1.0 + nrm(ks[2], (L, D), 0.02),
        'norm2_g': 1.0 + nrm(ks[3], (L, D), 0.02),
        'ada_w': nrm(ks[4], (L, D, 6 * D), 0.5 * D ** -0.5),
        'ada_b': nrm(ks[5], (L, 6 * D), 0.02),
        'w_in': nrm(ks[6], (L, D, N_IN), D ** -0.5),
        's5_lambda_re': -0.5 + nrm(ks[7], (L, G, P), 0.01),
        's5_lambda_im': math.pi * jnp.arange(P, dtype=f32) + nrm(ks[8], (L, G, P), 0.01),
        's5_log_dt': jax.random.uniform(ks[9], (L, G), f32, math.log(S5_DT_MIN), math.log(S5_DT_MAX)),
        's5_b_re': nrm(ks[10], (L, G, P, H), (2 * H) ** -0.5),
        's5_b_im': nrm(ks[11], (L, G, P, H), (2 * H) ** -0.5),
        's5_c_re': nrm(ks[12], (L, G, H, P), (2 * P) ** -0.5),
        's5_c_im': nrm(ks[13], (L, G, H, P), (2 * P) ** -0.5),
        's5_d': nrm(ks[14], (L, G, H), 1.0),
        's5_glu_w': nrm(ks[15], (L, GW, GW), GW ** -0.5),
        's5_glu_b': nrm(ks[16], (L, GW), 0.01),
        'swa_sinks': nrm(ks[17], (L, SWA_HEADS), 0.5),
        'mla_q_norm': 1.0 + nrm(ks[18], (L, MLA_Q_RANK), 0.02),
        'mla_kv_norm': 1.0 + nrm(ks[19], (L, MLA_KV_RANK), 0.02),
        'mla_w_uq': nrm(ks[20], (L, MLA_Q_RANK, MLA_HEADS * (MLA_NOPE + MLA_ROPE)), MLA_Q_RANK ** -0.5),
        'mla_w_ukv': nrm(ks[21], (L, MLA_KV_RANK, MLA_HEADS * (MLA_NOPE + MLA_V)), MLA_KV_RANK ** -0.5),
        'w_out': nrm(ks[22], (L, MIX_WIDTH, D), MIX_WIDTH ** -0.5),
        'mlp_w1': nrm(ks[23], (L, D, D_FF), D ** -0.5),
        'mlp_w2': nrm(ks[24], (L, D_FF, D), D_FF ** -0.5),
        'final_norm_g': 1.0 + nrm(ks[25], (D,), 0.02),
    }


def reference(x, c, norm1_g, norm2_g, ada_w, ada_b, w_in, s5_lambda_re, s5_lambda_im, s5_log_dt,
              s5_b_re, s5_b_im, s5_c_re, s5_c_im, s5_d, s5_glu_w, s5_glu_b, swa_sinks,
              mla_q_norm, mla_kv_norm, mla_w_uq, mla_w_ukv, w_out, mlp_w1, mlp_w2, final_norm_g):
    bsz, seq, _ = x.shape
    h = x
    c_act = jax.nn.silu(c)
    for l in range(DEPTH):
        mod = c_act @ ada_w[l] + ada_b[l]
        sh1, sc1, gt1, sh2, sc2, gt2 = [m[:, None, :] for m in jnp.split(mod, 6, axis=-1)]
        a = rms_norm(h, norm1_g[l]) * (1 + sc1) + sh1
        proj = a @ w_in[l]
        (u_s5, r_q, r_k, r_v, r_g, w_q, w_k, w_v,
         m_cq, m_ckv, m_kr) = jnp.split(proj, SPLIT_IDX, axis=-1)
        y_s5 = s5_mixer(u_s5, s5_lambda_re[l], s5_lambda_im[l], s5_log_dt[l], s5_b_re[l], s5_b_im[l],
                        s5_c_re[l], s5_c_im[l], s5_d[l], s5_glu_w[l], s5_glu_b[l])
        y_ret = retention_mixer(r_q.reshape(bsz, seq, RET_HEADS, RET_QK),
                                r_k.reshape(bsz, seq, RET_HEADS, RET_QK),
                                r_v.reshape(bsz, seq, RET_HEADS, RET_V), r_g)
        y_swa = swa_mixer(w_q.reshape(bsz, seq, SWA_HEADS, SWA_HD),
                          w_k.reshape(bsz, seq, SWA_KV_HEADS, SWA_HD),
                          w_v.reshape(bsz, seq, SWA_KV_HEADS, SWA_HD), swa_sinks[l])
        y_mla = mla_mixer(m_cq, m_ckv, m_kr, mla_q_norm[l], mla_kv_norm[l], mla_w_uq[l], mla_w_ukv[l])
        mixed = jnp.concatenate([y_s5, y_ret, y_swa, y_mla], axis=-1) @ w_out[l]
        h = h + gt1 * mixed
        a = rms_norm(h, norm2_g[l]) * (1 + sc2) + sh2
        h = h + gt2 * (jnp.square(jax.nn.relu(a @ mlp_w1[l])) @ mlp_w2[l])
    return rms_norm(h, final_norm_g)
```

```python
import functools
import math

import numpy as np
import jax
import jax.numpy as jnp
from jax import lax
from jax.experimental import pallas as pl
from jax.experimental.pallas import tpu as pltpu

F32 = jnp.float32
BF16 = jnp.bfloat16

D_MODEL = 2048
GROUP_WIDTH = 512
D_FF = 4 * D_MODEL
S5_CH = 16
S5_GROUPS = GROUP_WIDTH // S5_CH
S5_STATE = 64
RET_HEADS = 4
RET_QK = 64
RET_V = 128
SWA_HD = 64
SWA_HEADS = 8
SWA_KV_HEADS = 2
WINDOW = 128
MLA_HEADS = 4
MLA_Q_RANK = 384
MLA_KV_RANK = 128
MLA_NOPE = 128
MLA_ROPE = 64
MLA_V = 128
MLA_QK_PAD = 256
ROPE_BASE = 10000.0
EPS = 1e-6
NEG = -1e30

MIB = 1024 * 1024
S5_CHUNK = 32
RET_BLOCK = 256
SWA_BLOCK = 512
MLA_TQ = 512
MLA_TK = 512

_O_U, _O_RQ, _O_RK, _O_RV, _O_RG, _O_WQ, _O_WK, _O_WV, _O_CQ, _O_CKV, _O_KR, _O_END = (
    0, 512, 768, 1024, 1536, 2048, 2560, 2688, 2816, 3200, 3328, 3392)
_E_U, _E_RQ, _E_RK, _E_RV, _E_RG, _E_WQ, _E_WK2, _E_WV2, _E_CQ, _E_CKV, _E_KR, _E_KRS, _E_RQS, _E_RKS, _E_END = (
    0, 512, 768, 1024, 1536, 2048, 2560, 2816, 3072, 3456, 3584, 3712, 3840, 4096, 4352)


def _cparams(sem, vmem_mib):
    return pltpu.CompilerParams(dimension_semantics=sem, vmem_limit_bytes=int(vmem_mib * MIB))


def _resident(block_shape, index_map):
    return pl.BlockSpec(block_shape, index_map, pipeline_mode=pl.Buffered(1))


def _mod_kernel(c_ref, w_ref, b_ref, o_ref):
    c = c_ref[...]
    ca = (c * jax.nn.sigmoid(c)).astype(BF16)
    o_ref[...] = jnp.dot(ca, w_ref[...].astype(BF16), preferred_element_type=F32) + b_ref[...]


def _adaln_mod(c, ada_w, ada_b):
    depth, d, n = ada_w.shape
    bsz = c.shape[0]
    rows = 8
    cp = jnp.zeros((rows, d), F32).at[:bsz].set(c)
    tn = 1024
    out = pl.pallas_call(
        _mod_kernel,
        out_shape=jax.ShapeDtypeStruct((depth, rows, n), F32),
        grid=(depth, n // tn),
        in_specs=[
            pl.BlockSpec((rows, d), lambda l, j: (0, 0)),
            pl.BlockSpec((None, d, tn), lambda l, j: (l, 0, j)),
            pl.BlockSpec((None, 1, tn), lambda l, j: (l, 0, j)),
        ],
        out_specs=pl.BlockSpec((None, rows, tn), lambda l, j: (l, 0, j)),
        compiler_params=_cparams(("parallel", "parallel"), 40),
        name="adaln_mod",
    )(cp, ada_w, ada_b.reshape(depth, 1, n))
    return out[:, :bsz]


def _inproj_kernel(h_ref, sc_ref, sh_ref, g_ref, w_ref, cos_ref, sin_ref,
                   u_ref, rq_ref, rk_ref, rv_ref, rg_ref, wq_ref, wk_ref, wv_ref,
                   cq_ref, ckv_ref, kr_ref):
    x = h_ref[...]
    ms = jnp.mean(x * x, axis=-1, keepdims=True)
    y = x * lax.rsqrt(ms + EPS) * g_ref[...]
    a = (y * (1.0 + sc_ref[...]) + sh_ref[...]).astype(BF16)
    p = jnp.dot(a, w_ref[...], preferred_element_type=F32)
    cos = cos_ref[...]
    sin = sin_ref[...]
    u_ref[...] = p[:, _E_U:_E_RQ].astype(BF16)
    rq_ref[...] = (p[:, _E_RQ:_E_RK] * cos + p[:, _E_RQS:_E_RKS] * sin).astype(BF16)
    rk_ref[...] = (p[:, _E_RK:_E_RV] * cos + p[:, _E_RKS:_E_END] * sin).astype(BF16)
    rv_ref[...] = p[:, _E_RV:_E_RG].astype(BF16)
    rg_ref[...] = p[:, _E_RG:_E_WQ].astype(BF16)
    wq_ref[...] = p[:, _E_WQ:_E_WK2].astype(BF16)
    wk_ref[...] = p[:, _E_WK2:_E_WV2].astype(BF16)
    wv_ref[...] = p[:, _E_WV2:_E_CQ].astype(BF16)
    cq_ref[...] = p[:, _E_CQ:_E_CKV].astype(BF16)
    ckv_ref[...] = p[:, _E_CKV:_E_KR].astype(BF16)
    kr = p[:, _E_KR:_E_KRS] * cos[:, :128] + p[:, _E_KRS:_E_RQS] * sin[:, :128]
    kr_ref[:, 0:128] = jnp.zeros_like(kr).astype(BF16)
    kr_ref[:, 128:256] = kr.astype(BF16)


def _in_projection(h2, sc, sh, gain, w_ext, cos4, sin4, seq):
    tokens, d = h2.shape
    tm = 512
    per_seq = seq // tm
    widths = (512, 256, 256, 512, 512, 512, 256, 256, 384, 128, 256)
    row_blk = lambda w: pl.BlockSpec((tm, w), lambda i: (i, 0))
    return pl.pallas_call(
        _inproj_kernel,
        out_shape=tuple(jax.ShapeDtypeStruct((tokens, w), BF16) for w in widths),
        grid=(tokens // tm,),
        in_specs=[
            row_blk(d),
            pl.BlockSpec((None, 1, d), lambda i: (i // per_seq, 0, 0)),
            pl.BlockSpec((None, 1, d), lambda i: (i // per_seq, 0, 0)),
            pl.BlockSpec((1, d), lambda i: (0, 0)),
            _resident((d, _E_END), lambda i: (0, 0)),
            pl.BlockSpec((tm, 256), lambda i: (i % per_seq, 0)),
            pl.BlockSpec((tm, 256), lambda i: (i % per_seq, 0)),
        ],
        out_specs=tuple(row_blk(w) for w in widths),
        compiler_params=_cparams(("parallel",), 56),
        name="in_projection",
    )(h2, sc, sh, gain, w_ext, cos4, sin4)


def _s5_prep_kernel(lr_ref, li_ref, ldt_ref, btr_ref, bti_ref, cr_ref, ci_ref,
                    tmat_ref, min_ref, moutt_ref, pa_ref, pb_ref, *, chunk, nlev):
    width = chunk * S5_CH
    lr = lr_ref[...]
    li = li_ref[...]
    dt = jnp.exp(ldt_ref[...])
    btr, bti = btr_ref[...], bti_ref[...]
    cre, cim = cr_ref[...], ci_ref[...]

    def cpow(k):
        mag = jnp.exp(lr * dt * k)
        ang = li * dt * k
        return mag * jnp.cos(ang), mag * jnp.sin(ang)

    one = jnp.ones((1, 1), F32)
    ar, ai = cpow(one)
    den = lr * lr + li * li
    kr = ((ar - 1.0) * lr + ai * li) / den
    ki = (ai * lr - (ar - 1.0) * li) / den
    bbr = kr * btr - ki * bti
    bbi = kr * bti + ki * btr

    kidx = lax.broadcasted_iota(jnp.int32, (chunk, 1), 0).astype(F32)

    def outer(pr, pi, mr, mi):
        re = pr[:, None, :] * mr[None, :, :] - pi[:, None, :] * mi[None, :, :]
        im = pr[:, None, :] * mi[None, :, :] + pi[:, None, :] * mr[None, :, :]
        return re.reshape(width, S5_STATE), im.reshape(width, S5_STATE)

    er, ei = cpow((chunk - 1.0) - kidx)
    m_r, m_i = outer(er, ei, bbr, bbi)
    min_ref[...] = jnp.concatenate([m_r, m_i], axis=1).astype(BF16)

    fr, fi = cpow(kidx + 1.0)
    w_r, w_i = outer(fr, fi, cre, cim)
    moutt_ref[...] = jnp.concatenate([w_r, -w_i], axis=1).astype(BF16)

    gr, gi = cpow(kidx)
    g_r, g_i = outer(gr, gi, cre, cim)
    g0 = jnp.concatenate([g_r, -g_i], axis=1)
    bb = jnp.concatenate([bbr, bbi], axis=1)
    kt = lax.dot_general(bb, g0, (((1,), (1,)), ((), ())), precision=lax.Precision.HIGHEST,
                         preferred_element_type=F32)
    lane = lax.broadcasted_iota(jnp.int32, (S5_CH, width), 1)
    for s in range(chunk):
        if s == 0:
            rows = kt
        else:
            rows = jnp.where(lane >= S5_CH * s, pltpu.roll(kt, S5_CH * s, axis=1), 0.0)
        tmat_ref[S5_CH * s:S5_CH * (s + 1), :] = rows.astype(BF16)

    lev = lax.broadcasted_iota(jnp.int32, (8, 1), 0)
    pr, pi = cpow((chunk * jnp.left_shift(1, jnp.minimum(lev, nlev))).astype(F32))
    pa_ref[...] = jnp.concatenate([pr, pr], axis=1)
    pb_ref[...] = jnp.concatenate([-pi, pi], axis=1)


def _s5_prepare(lam_re, lam_im, log_dt, b_re, b_im, c_re, c_im, chunk, nlev):
    g, p = lam_re.shape
    h = S5_CH
    width = chunk * h
    vec = lambda: pl.BlockSpec((None, 1, p), lambda i: (i, 0, 0))
    mat = lambda: pl.BlockSpec((None, h, p), lambda i: (i, 0, 0))
    return pl.pallas_call(
        functools.partial(_s5_prep_kernel, chunk=chunk, nlev=nlev),
        out_shape=(
            jax.ShapeDtypeStruct((g, width, width), BF16),
            jax.ShapeDtypeStruct((g, width, 2 * p), BF16),
            jax.ShapeDtypeStruct((g, width, 2 * p), BF16),
            jax.ShapeDtypeStruct((g, 8, 2 * p), F32),
            jax.ShapeDtypeStruct((g, 8, 2 * p), F32),
        ),
        grid=(g,),
        in_specs=[vec(), vec(), vec(), mat(), mat(), mat(), mat()],
        out_specs=(
            pl.BlockSpec((None, width, width), lambda i: (i, 0, 0)),
            pl.BlockSpec((None, width, 2 * p), lambda i: (i, 0, 0)),
            pl.BlockSpec((None, width, 2 * p), lambda i: (i, 0, 0)),
            pl.BlockSpec((None, 8, 2 * p), lambda i: (i, 0, 0)),
            pl.BlockSpec((None, 8, 2 * p), lambda i: (i, 0, 0)),
        ),
        compiler_params=_cparams(("parallel",), 32),
        name="s5_prepare",
    )(lam_re.reshape(g, 1, p), lam_im.reshape(g, 1, p),
      jnp.broadcast_to(log_dt[:, None, None], (g, 1, p)),
      jnp.swapaxes(b_re, 1, 2), jnp.swapaxes(b_im, 1, 2), c_re, c_im)


def _s5_scan_kernel(u_ref, t_ref, min_ref, moutt_ref, pa_ref, pb_ref, d_ref, y_ref, *, ncb, nlev):
    u = u_ref[...]
    y_intra = jnp.dot(u, t_ref[...], preferred_element_type=F32)
    x = jnp.dot(u, min_ref[...], preferred_element_type=F32)
    rows = x.shape[0]
    cidx = lax.broadcasted_iota(jnp.int32, (rows, 1), 0) % ncb
    for k in range(nlev):
        d = 1 << k
        sh = jnp.where(cidx >= d, pltpu.roll(x, d, axis=0), 0.0)
        shs = pltpu.roll(sh, S5_STATE, axis=1)
        x = x + pa_ref[k:k + 1, :] * sh + pb_ref[k:k + 1, :] * shs
    xp = jnp.where(cidx >= 1, pltpu.roll(x, 1, axis=0), 0.0)
    y_cross = lax.dot_general(xp.astype(BF16), moutt_ref[...], (((1,), (1,)), ((), ())),
                              preferred_element_type=F32)
    y_ref[...] = y_intra + y_cross + u.astype(F32) * d_ref[...]


def _s5_scan(u_g, tmat, min_m, moutt, pa, pb, d_tiled, ncb, nlev):
    g, rows, width = u_g.shape
    p2 = min_m.shape[-1]
    return pl.pallas_call(
        functools.partial(_s5_scan_kernel, ncb=ncb, nlev=nlev),
        out_shape=jax.ShapeDtypeStruct((g, rows, width), F32),
        grid=(g,),
        in_specs=[
            pl.BlockSpec((None, rows, width), lambda i: (i, 0, 0)),
            pl.BlockSpec((None, width, width), lambda i: (i, 0, 0)),
            pl.BlockSpec((None, width, p2), lambda i: (i, 0, 0)),
            pl.BlockSpec((None, width, p2), lambda i: (i, 0, 0)),
            pl.BlockSpec((None, 8, p2), lambda i: (i, 0, 0)),
            pl.BlockSpec((None, 8, p2), lambda i: (i, 0, 0)),
            pl.BlockSpec((None, 1, width), lambda i: (i, 0, 0)),
        ],
        out_specs=pl.BlockSpec((None, rows, width), lambda i: (i, 0, 0)),
        compiler_params=_cparams(("parallel",), 32),
        name="s5_scan",
    )(u_g, tmat, min_m, moutt, pa, pb, d_tiled)


def _s5_mixer_pre_glu(u, lam_re, lam_im, log_dt, b_re, b_im, c_re, c_im, d_skip, bsz, seq):
    chunk = S5_CHUNK
    ncb = seq // chunk
    nlev = max(1, (ncb - 1).bit_length())
    g, h = S5_GROUPS, S5_CH
    tmat, min_m, moutt, pa, pb = _s5_prepare(lam_re, lam_im, log_dt, b_re, b_im, c_re, c_im, chunk, nlev)
    u_g = u.reshape(bsz * ncb, chunk, g, h).transpose(2, 0, 1, 3).reshape(g, bsz * ncb, chunk * h)
    d_tiled = jnp.tile(d_skip, (1, chunk)).reshape(g, 1, chunk * h)
    y_g = _s5_scan(u_g, tmat, min_m, moutt, pa, pb, d_tiled, ncb, nlev)
    return y_g.reshape(g, bsz * ncb, chunk, h).transpose(1, 2, 0, 3).reshape(bsz * seq, g * h)


def _ret_kernel(q_ref, k_ref, v_ref, g_ref, dec_ref, xi_ref, zeta_ref, gc_ref, bd_ref, o_ref, st_ref):
    @pl.when(pl.program_id(1) == 0)
    def _():
        st_ref[...] = jnp.zeros_like(st_ref)

    q = q_ref[...]
    k = k_ref[...]
    v = v_ref[...]
    lane_head = lax.broadcasted_iota(jnp.int32, (1, RET_HEADS * RET_QK), 1) // RET_QK
    st = st_ref[...]
    o_cross = jnp.dot(q, st.astype(BF16), preferred_element_type=F32) * xi_ref[...]
    outs = []
    for hd in range(RET_HEADS):
        km = jnp.where(lane_head == hd, k, jnp.zeros_like(k))
        s = lax.dot_general(q, km, (((1,), (1,)), ((), ())), preferred_element_type=F32)
        p = (s * dec_ref[hd]).astype(BF16)
        vh = v[:, hd * RET_V:(hd + 1) * RET_V]
        o = jnp.dot(p, vh, preferred_element_type=F32) + o_cross[:, hd * RET_V:(hd + 1) * RET_V]
        outs.append(o * lax.rsqrt(jnp.mean(o * o, axis=-1, keepdims=True) + EPS))
    on = jnp.concatenate(outs, axis=1)
    gate = g_ref[...].astype(F32)
    o_ref[...] = (on * (gate * jax.nn.sigmoid(gate))).astype(BF16)

    kz = (k.astype(F32) * zeta_ref[...]).astype(BF16)
    kv = lax.dot_general(kz, v, (((0,), (0,)), ((), ())), preferred_element_type=F32)
    st_ref[...] = st * gc_ref[...] + kv * bd_ref[...]


def _retention(rq, rk, rv, rg, bsz, seq):
    cb = min(RET_BLOCK, seq)
    nblk = seq // cb
    hq = RET_HEADS * RET_QK
    log_gamma = jnp.log1p(-(2.0 ** (-5.0 - jnp.arange(RET_HEADS, dtype=F32))))
    idx = jnp.arange(cb, dtype=F32)
    rel = idx[:, None] - idx[None, :]
    decay = jnp.where(rel >= 0, jnp.exp(log_gamma[:, None, None] * jnp.maximum(rel, 0.0)), 0.0)
    xi = jnp.repeat(jnp.exp(log_gamma[None, :] * (idx[:, None] + 1.0)), RET_V, axis=1)
    zeta = jnp.repeat(jnp.exp(log_gamma[None, :] * (cb - 1.0 - idx[:, None])), RET_QK, axis=1)
    gchunk = jnp.repeat(jnp.exp(log_gamma * cb), RET_V)[None, :]
    blockdiag = (jnp.arange(hq)[:, None] // RET_QK == jnp.arange(GROUP_WIDTH)[None, :] // RET_V).astype(F32)
    tok = lambda w: pl.BlockSpec((cb, w), lambda b, c: (b * nblk + c, 0))
    const = lambda shape: pl.BlockSpec(shape, lambda b, c: (0,) * len(shape))
    return pl.pallas_call(
        _ret_kernel,
        out_shape=jax.ShapeDtypeStruct((bsz * seq, GROUP_WIDTH), BF16),
        grid=(bsz, nblk),
        in_specs=[tok(hq), tok(hq), tok(GROUP_WIDTH), tok(GROUP_WIDTH),
                  const((RET_HEADS, cb, cb)), const((cb, GROUP_WIDTH)), const((cb, hq)),
                  const((1, GROUP_WIDTH)), const((hq, GROUP_WIDTH))],
        out_specs=tok(GROUP_WIDTH),
        scratch_shapes=[pltpu.VMEM((hq, GROUP_WIDTH), F32)],
        compiler_params=_cparams(("parallel", "arbitrary"), 32),
        name="retention",
    )(rq, rk, rv, rg, decay, xi, zeta, gchunk, blockdiag)


def _swa_kernel(sink_ref, q_ref, kc_ref, kp_ref, vc_ref, vp_ref, o_ref, *, nsub):
    blk = pl.program_id(1)
    kall = jnp.concatenate([kp_ref[...], kc_ref[...]], axis=0)
    vall = jnp.concatenate([vp_ref[...], vc_ref[...]], axis=0)
    half = lax.broadcasted_iota(jnp.int32, (1, 2 * SWA_HD), 1) // SWA_HD
    r = lax.broadcasted_iota(jnp.int32, (WINDOW, 2 * WINDOW), 0)
    jc = lax.broadcasted_iota(jnp.int32, (WINDOW, 2 * WINDOW), 1)
    dist = r + WINDOW - jc
    band = jnp.logical_and(dist >= 0, dist < WINDOW)
    first_key = jnp.where(blk > 0, 0, WINDOW)
    band_first = jnp.logical_and(band, jc >= first_key)
    for sb in range(nsub):
        valid = band_first if sb == 0 else band
        r0 = sb * WINDOW
        for j in range(SWA_KV_HEADS):
            kk = kall[r0:r0 + 2 * WINDOW, 128 * j:128 * (j + 1)]
            vv = vall[r0:r0 + 2 * WINDOW, 128 * j:128 * (j + 1)]
            zk = jnp.zeros_like(kk)
            kcat = jnp.concatenate([jnp.where(half == 0, kk, zk), jnp.where(half == 1, kk, zk)], axis=0)
            vcat = jnp.concatenate([jnp.where(half == 0, vv, zk), jnp.where(half == 1, vv, zk)], axis=0)
            q2 = jnp.concatenate([q_ref[r0:r0 + WINDOW, 256 * j:256 * j + 128],
                                  q_ref[r0:r0 + WINDOW, 256 * j + 128:256 * j + 256]], axis=0)
            s_all = lax.dot_general(q2, kcat, (((1,), (1,)), ((), ())), preferred_element_type=F32)
            p_rows, inv_rows = [], []
            for a in range(2):
                p_cols, invs = [], []
                for e in range(2):
                    sink = sink_ref[4 * j + 2 * a + e]
                    s = s_all[WINDOW * a:WINDOW * (a + 1), 2 * WINDOW * e:2 * WINDOW * (e + 1)]
                    s = jnp.where(valid, s, NEG)
                    m = jnp.maximum(jnp.max(s, axis=-1, keepdims=True), sink)
                    p = jnp.exp(s - m)
                    denom = jnp.sum(p, axis=-1, keepdims=True) + jnp.exp(sink - m)
                    p_cols.append(p.astype(BF16))
                    invs.append(1.0 / denom)
                p_rows.append(jnp.concatenate(p_cols, axis=1))
                inv_rows.append(jnp.where(half == 0, invs[0], invs[1]))
            p_all = jnp.concatenate(p_rows, axis=0)
            o = jnp.dot(p_all, vcat, preferred_element_type=F32)
            for a in range(2):
                o_ref[r0:r0 + WINDOW, 256 * j + 128 * a:256 * j + 128 * (a + 1)] = (
                    o[WINDOW * a:WINDOW * (a + 1)] * inv_rows[a]).astype(BF16)


def _swa(wq, wk2, wv2, sinks, bsz, seq):
    qb = min(SWA_BLOCK, seq)
    nblk = seq // qb
    per = qb // WINDOW
    cur = lambda w: pl.BlockSpec((qb, w), lambda b, i: (b * nblk + i, 0))
    prev = lambda w: pl.BlockSpec((WINDOW, w), lambda b, i: (jnp.maximum((b * nblk + i) * per - 1, 0), 0))
    return pl.pallas_call(
        functools.partial(_swa_kernel, nsub=per),
        out_shape=jax.ShapeDtypeStruct((bsz * seq, GROUP_WIDTH), BF16),
        grid=(bsz, nblk),
        in_specs=[pl.BlockSpec(memory_space=pltpu.SMEM),
                  cur(GROUP_WIDTH), cur(256), prev(256), cur(256), prev(256)],
        out_specs=cur(GROUP_WIDTH),
        compiler_params=_cparams(("parallel", "parallel"), 32),
        name="swa",
    )(sinks, wq, wk2, wk2, wv2, wv2)


def _mla_up_kernel(cq_ref, ckv_ref, kr_ref, qn_ref, kvn_ref, wq_ref, wkv_ref, cos_ref, sin_ref,
                   q_ref, k_ref, v_ref):
    cq = cq_ref[...].astype(F32)
    nq = (cq * lax.rsqrt(jnp.mean(cq * cq, axis=-1, keepdims=True) + EPS) * qn_ref[...]).astype(BF16)
    qq = jnp.dot(nq, wq_ref[...], preferred_element_type=F32)
    ckv = ckv_ref[...].astype(F32)
    nkv = (ckv * lax.rsqrt(jnp.mean(ckv * ckv, axis=-1, keepdims=True) + EPS) * kvn_ref[...]).astype(BF16)
    kv = jnp.dot(nkv, wkv_ref[...], preferred_element_type=F32)
    cos, sin = cos_ref[...], sin_ref[...]
    kr = kr_ref[...].astype(F32)
    hw = MLA_QK_PAD
    for hd in range(MLA_HEADS):
        q_ref[:, hw * hd:hw * (hd + 1)] = (
            qq[:, hw * hd:hw * (hd + 1)] * cos
            + qq[:, hw * (MLA_HEADS + hd):hw * (MLA_HEADS + hd + 1)] * sin).astype(BF16)
        k_ref[:, hw * hd:hw * (hd + 1)] = (kv[:, hw * hd:hw * (hd + 1)] + kr).astype(BF16)
    v_ref[...] = kv[:, hw * MLA_HEADS:].astype(BF16)


def _mla_up(cq, ckv, kr_pad, q_norm, kv_norm, wq_ext, wkv_ext, cosq, sinq, seq):
    tokens = cq.shape[0]
    tm = min(512, seq)
    per_seq = seq // tm
    hw = MLA_QK_PAD
    row = lambda w: pl.BlockSpec((tm, w), lambda i: (i, 0))
    tab = lambda: pl.BlockSpec((tm, hw), lambda i: (i % per_seq, 0))
    full = lambda shape: pl.BlockSpec(shape, lambda i: (0, 0))
    return pl.pallas_call(
        _mla_up_kernel,
        out_shape=(jax.ShapeDtypeStruct((tokens, MLA_HEADS * hw), BF16),
                   jax.ShapeDtypeStruct((tokens, MLA_HEADS * hw), BF16),
                   jax.ShapeDtypeStruct((tokens, MLA_HEADS * MLA_V), BF16)),
        grid=(tokens // tm,),
        in_specs=[row(MLA_Q_RANK), row(MLA_KV_RANK), row(hw), full((1, MLA_Q_RANK)), full((1, MLA_KV_RANK)),
                  full(wq_ext.shape), full(wkv_ext.shape), tab(), tab()],
        out_specs=(row(MLA_HEADS * hw), row(MLA_HEADS * hw), row(MLA_HEADS * MLA_V)),
        compiler_params=_cparams(("parallel",), 40),
        name="mla_up",
    )(cq, ckv, kr_pad, q_norm, kv_norm, wq_ext, wkv_ext, cosq, sinq)


def _mla_attn_kernel(q_ref, k_ref, v_ref, o_ref, m_ref, l_ref, acc_ref, *, tq, tk):
    qi = pl.program_id(2)
    q = q_ref[...]
    m_ref[...] = jnp.full_like(m_ref, NEG)
    l_ref[...] = jnp.zeros_like(l_ref)
    acc_ref[...] = jnp.zeros_like(acc_ref)

    def step(start, masked):
        ks = k_ref[pl.ds(start, tk), :]
        vs = v_ref[pl.ds(start, tk), :]
        s = lax.dot_general(q, ks, (((1,), (1,)), ((), ())), preferred_element_type=F32)
        if masked:
            rr = lax.broadcasted_iota(jnp.int32, (tq, tk), 0)
            cc = lax.broadcasted_iota(jnp.int32, (tq, tk), 1)
            s = jnp.where(cc <= rr, s, NEG)
        m_old = m_ref[...]
        m_new = jnp.maximum(m_old, jnp.max(s, axis=-1, keepdims=True))
        alpha = jnp.exp(m_old - m_new)
        p = jnp.exp(s - m_new)
        l_ref[...] = alpha * l_ref[...] + jnp.sum(p, axis=-1, keepdims=True)
        acc_ref[...] = alpha * acc_ref[...] + jnp.dot(p.astype(BF16), vs, preferred_element_type=F32)
        m_ref[...] = m_new

    def body(kb, carry):
        step(pl.multiple_of(kb * tk, tk), False)
        return carry

    lax.fori_loop(0, qi, body, 0)
    step(pl.multiple_of(qi * tk, tk), True)
    o_ref[...] = (acc_ref[...] / l_ref[...]).astype(BF16)


def _mla_attention(q, k, v, bsz, seq):
    tq = tk = min(MLA_TQ, seq)
    nq = seq // tq
    hw = MLA_QK_PAD
    return pl.pallas_call(
        functools.partial(_mla_attn_kernel, tq=tq, tk=tk),
        out_shape=jax.ShapeDtypeStruct((bsz * seq, MLA_HEADS * MLA_V), BF16),
        grid=(bsz, MLA_HEADS, nq),
        in_specs=[pl.BlockSpec((tq, hw), lambda b, h, i: (b * nq + i, h)),
                  pl.BlockSpec((seq, hw), lambda b, h, i: (b, h)),
                  pl.BlockSpec((seq, MLA_V), lambda b, h, i: (b, h))],
        out_specs=pl.BlockSpec((tq, MLA_V), lambda b, h, i: (b * nq + i, h)),
        scratch_shapes=[pltpu.VMEM((tq, 1), F32), pltpu.VMEM((tq, 1), F32), pltpu.VMEM((tq, MLA_V), F32)],
        compiler_params=_cparams(("parallel", "parallel", "arbitrary"), 40),
        name="mla_attention",
    )(q, k, v)


def _outproj_kernel(ys5_ref, yret_ref, yswa_ref, ymla_ref, h_ref, gt_ref, gw_ref, gb_ref, w_ref, o_ref):
    y = ys5_ref[...]
    z = jax.nn.gelu(y, approximate=True)
    gl = jnp.dot(z.astype(BF16), gw_ref[...], preferred_element_type=F32) + gb_ref[...]
    s5 = (z * jax.nn.sigmoid(gl)).astype(BF16)
    gw = GROUP_WIDTH
    mixed = jnp.dot(s5, w_ref[0:gw, :], preferred_element_type=F32)
    mixed += jnp.dot(yret_ref[...], w_ref[gw:2 * gw, :], preferred_element_type=F32)
    mixed += jnp.dot(yswa_ref[...], w_ref[2 * gw:3 * gw, :], preferred_element_type=F32)
    mixed += jnp.dot(ymla_ref[...], w_ref[3 * gw:4 * gw, :], preferred_element_type=F32)
    o_ref[...] = h_ref[...] + gt_ref[...] * mixed


def _out_projection(ys5, yret, yswa, ymla, h2, gt, glu_w, glu_b, w_out, seq):
    tokens, d = h2.shape
    tm = min(512, seq)
    per_seq = seq // tm
    gw = GROUP_WIDTH
    row = lambda w: pl.BlockSpec((tm, w), lambda i: (i, 0))
    return pl.pallas_call(
        _outproj_kernel,
        out_shape=jax.ShapeDtypeStruct((tokens, d), F32),
        grid=(tokens // tm,),
        in_specs=[row(gw), row(gw), row(gw), row(gw), row(d),
                  pl.BlockSpec((None, 1, d), lambda i: (i // per_seq, 0, 0)),
                  pl.BlockSpec((gw, gw), lambda i: (0, 0)),
                  pl.BlockSpec((1, gw), lambda i: (0, 0)),
                  _resident((4 * gw, d), lambda i: (0, 0))],
        out_specs=row(d),
        compiler_params=_cparams(("parallel",), 48),
        name="out_projection",
    )(ys5, yret, yswa, ymla, h2, gt, glu_w, glu_b, w_out)


def _mlp_kernel(h_ref, sc_ref, sh_ref, gt_ref, g_ref, fg_ref, w1_ref, w2_ref, o_ref, a_ref, *, final):
    j = pl.program_id(1)

    @pl.when(j == 0)
    def _():
        x = h_ref[...]
        ms = jnp.mean(x * x, axis=-1, keepdims=True)
        y = x * lax.rsqrt(ms + EPS) * g_ref[...]
        a_ref[...] = (y * (1.0 + sc_ref[...]) + sh_ref[...]).astype(BF16)

    hid = jnp.dot(a_ref[...], w1_ref[...], preferred_element_type=F32)
    hid = jnp.square(jnp.maximum(hid, 0.0)).astype(BF16)
    part = jnp.dot(hid, w2_ref[...], preferred_element_type=F32)

    @pl.when(j == 0)
    def _():
        o_ref[...] = part

    @pl.when(j > 0)
    def _():
        o_ref[...] += part

    @pl.when(j == pl.num_programs(1) - 1)
    def _():
        out = h_ref[...] + gt_ref[...] * o_ref[...]
        if final:
            ms = jnp.mean(out * out, axis=-1, keepdims=True)
            out = out * lax.rsqrt(ms + EPS) * fg_ref[...]
        o_ref[...] = out


def _mlp(h2, sc, sh, gt, gain, final_gain, w1, w2, seq, final):
    tokens, d = h2.shape
    dff = w1.shape[1]
    tm = min(512, seq)
    tf = 512
    per_seq = seq // tm
    modv = lambda: pl.BlockSpec((None, 1, d), lambda i, j: (i // per_seq, 0, 0))
    return pl.pallas_call(
        functools.partial(_mlp_kernel, final=final),
        out_shape=jax.ShapeDtypeStruct((tokens, d), F32),
        grid=(tokens // tm, dff // tf),
        in_specs=[pl.BlockSpec((tm, d), lambda i, j: (i, 0)),
                  modv(), modv(), modv(),
                  pl.BlockSpec((1, d), lambda i, j: (0, 0)),
                  pl.BlockSpec((1, d), lambda i, j: (0, 0)),
                  pl.BlockSpec((d, tf), lambda i, j: (0, j)),
                  pl.BlockSpec((tf, d), lambda i, j: (j, 0))],
        out_specs=pl.BlockSpec((tm, d), lambda i, j: (i, 0)),
        scratch_shapes=[pltpu.VMEM((tm, d), BF16)],
        compiler_params=_cparams(("parallel", "arbitrary"), 48),
        name="mlp",
    )(h2, sc, sh, gt, gain, final_gain, w1, w2)


def _swap_halves(w, heads, hd):
    rows = w.shape[0]
    w4 = w.reshape(rows, heads, 2, hd // 2)
    return jnp.concatenate([w4[:, :, 1:], w4[:, :, :1]], axis=2).reshape(rows, heads * hd)


def _in_weight_ext(w_in_l):
    d = w_in_l.shape[0]
    col = lambda a, b: w_in_l[:, a:b]
    z64 = jnp.zeros((d, 64), w_in_l.dtype)
    k_scale = RET_QK ** -0.5
    kr = col(_O_KR, _O_END)
    parts = [
        col(_O_U, _O_RK),
        col(_O_RK, _O_RV) * k_scale,
        col(_O_RV, _O_WQ),
        col(_O_WQ, _O_WK) * (SWA_HD ** -0.5),
        col(_O_WK, _O_WK + 64), col(_O_WK, _O_WK + 64), col(_O_WK + 64, _O_WV), col(_O_WK + 64, _O_WV),
        col(_O_WV, _O_WV + 64), col(_O_WV, _O_WV + 64), col(_O_WV + 64, _O_CQ), col(_O_WV + 64, _O_CQ),
        col(_O_CQ, _O_KR),
        kr, z64,
        _swap_halves(kr, 1, MLA_ROPE), z64,
        _swap_halves(col(_O_RQ, _O_RK), RET_HEADS, RET_QK),
        _swap_halves(col(_O_RK, _O_RV), RET_HEADS, RET_QK) * k_scale,
    ]
    w_ext = jnp.concatenate(parts, axis=1).astype(BF16)
    assert w_ext.shape[1] == _E_END
    return w_ext


def _mla_weight_ext(w_uq_l, w_ukv_l):
    rq, rkv = w_uq_l.shape[0], w_ukv_l.shape[0]
    hq = MLA_NOPE + MLA_ROPE
    hk = MLA_NOPE + MLA_V
    scale = hq ** -0.5
    zq64 = jnp.zeros((rq, 64), w_uq_l.dtype)
    zq128 = jnp.zeros((rq, 128), w_uq_l.dtype)
    zk128 = jnp.zeros((rkv, 128), w_ukv_l.dtype)
    plain, swapped, kcols, vcols = [], [], [], []
    for h in range(MLA_HEADS):
        plain += [w_uq_l[:, h * hq:(h + 1) * hq], zq64]
        swapped += [zq128, _swap_halves(w_uq_l[:, h * hq + MLA_NOPE:(h + 1) * hq], 1, MLA_ROPE), zq64]
        kcols += [w_ukv_l[:, h * hk:h * hk + MLA_NOPE], zk128]
        vcols += [w_ukv_l[:, h * hk + MLA_NOPE:(h + 1) * hk]]
    wq_ext = (jnp.concatenate(plain + swapped, axis=1) * scale).astype(BF16)
    wkv_ext = jnp.concatenate(kcols + vcols, axis=1).astype(BF16)
    return wq_ext, wkv_ext


def _rotary_tables(seq):
    d = RET_QK
    inv = ROPE_BASE ** (-jnp.arange(0, d, 2, dtype=F32) / d)
    ang = jnp.arange(seq, dtype=F32)[:, None] * inv[None, :]
    cos, sin = jnp.cos(ang), jnp.sin(ang)
    cos1 = jnp.concatenate([cos, cos], axis=1)
    sin1 = jnp.concatenate([-sin, sin], axis=1)
    cos4, sin4 = jnp.tile(cos1, (1, 4)), jnp.tile(sin1, (1, 4))
    zeros64 = jnp.zeros((seq, 64), F32)
    cosq = jnp.concatenate([jnp.ones((seq, MLA_NOPE), F32), cos1, zeros64], axis=1)
    sinq = jnp.concatenate([jnp.zeros((seq, MLA_NOPE), F32), sin1, zeros64], axis=1)
    return cos4, sin4, cosq, sinq


def kernel(x, c, norm1_g, norm2_g, ada_w, ada_b, w_in, s5_lambda_re, s5_lambda_im, s5_log_dt, s5_b_re, s5_b_im, s5_c_re, s5_c_im, s5_d, s5_glu_w, s5_glu_b, swa_sinks, mla_q_norm, mla_kv_norm, mla_w_uq, mla_w_ukv, w_out, mlp_w1, mlp_w2, final_norm_g):
    bsz, seq, d = x.shape
    depth = ada_w.shape[0]
    tokens = bsz * seq
    h = x.reshape(tokens, d)
    mod = _adaln_mod(c, ada_w, ada_b)
    cos4, sin4, cosq, sinq = _rotary_tables(seq)
    for l in range(depth):
        sh1, sc1, gt1, sh2, sc2, gt2 = [m.reshape(bsz, 1, d) for m in jnp.split(mod[l], 6, axis=-1)]
        w_ext = _in_weight_ext(w_in[l])
        (u, rq, rk, rv, rg, wq, wk2, wv2, cq, ckv, kr_pad) = _in_projection(
            h, sc1, sh1, norm1_g[l].reshape(1, d), w_ext, cos4, sin4, seq)
        y_s5 = _s5_mixer_pre_glu(u, s5_lambda_re[l], s5_lambda_im[l], s5_log_dt[l], s5_b_re[l], s5_b_im[l],
                                 s5_c_re[l], s5_c_im[l], s5_d[l], bsz, seq)
        y_ret = _retention(rq, rk, rv, rg, bsz, seq)
        y_swa = _swa(wq, wk2, wv2, swa_sinks[l], bsz, seq)
        wq_ext, wkv_ext = _mla_weight_ext(mla_w_uq[l], mla_w_ukv[l])
        mq, mk, mv = _mla_up(cq, ckv, kr_pad, mla_q_norm[l].reshape(1, -1), mla_kv_norm[l].reshape(1, -1),
                             wq_ext, wkv_ext, cosq, sinq, seq)
        y_mla = _mla_attention(mq, mk, mv, bsz, seq)
        h = _out_projection(y_s5, y_ret, y_swa, y_mla, h, gt1, s5_glu_w[l].astype(BF16),
                            s5_glu_b[l].reshape(1, -1), w_out[l].astype(BF16), seq)
        h = _mlp(h, sc2, sh2, gt2, norm2_g[l].reshape(1, d), final_norm_g.reshape(1, d),
                 mlp_w1[l].astype(BF16), mlp_w2[l].astype(BF16), seq, final=(l == depth - 1))
    return h.reshape(bsz, seq, d)
```

```python
import functools
import math

import jax
import jax.numpy as jnp
from jax import lax
from jax.experimental import pallas as pl
from jax.experimental.pallas import tpu as pltpu

F32 = jnp.float32
BF16 = jnp.bfloat16

D_MODEL = 2048
GROUP_WIDTH = 512
S5_CH = 16
S5_GROUPS = GROUP_WIDTH // S5_CH
S5_STATE = 64
RET_HEADS = 4
RET_QK = 64
RET_V = 128
SWA_HD = 64
SWA_HEADS = 8
SWA_KV_HEADS = 2
WINDOW = 128
MLA_HEADS = 4
MLA_Q_RANK = 384
MLA_KV_RANK = 128
MLA_NOPE = 128
MLA_ROPE = 64
MLA_V = 128
MLA_QK_PAD = 256
ROPE_BASE = 10000.0
EPS = 1e-6
NEG = -1e30

MIB = 1024 * 1024
LANES = 128
MOD_ROWS = 8
S5_CHUNK = 32
S5_GROUPS_PER_STEP = LANES // S5_CH
S5_BATCH_PER_STEP = 2
RET_BLOCK = 256
SWA_BLOCK = 512
MLA_TQ = 512
MLA_TK = 512
MLA_HEADS_PER_STEP = 4
MOD_SH1, MOD_SC1, MOD_GT1, MOD_SH2, MOD_SC2, MOD_GT2 = range(6)

_O_U, _O_RQ, _O_RK, _O_RV, _O_RG, _O_WQ, _O_WK, _O_WV, _O_CQ, _O_CKV, _O_KR, _O_END = (
    0, 512, 768, 1024, 1536, 2048, 2560, 2688, 2816, 3200, 3328, 3392)
_E_U, _E_RQ, _E_RK, _E_RV, _E_RG, _E_WQ, _E_WK2, _E_WV2, _E_CQ, _E_CKV, _E_KR, _E_KRS, _E_RQS, _E_RKS, _E_END = (
    0, 512, 768, 1024, 1536, 2048, 2560, 2816, 3072, 3456, 3584, 3712, 3840, 4096, 4352)


def _cparams(sem, vmem_mib):
    return pltpu.CompilerParams(dimension_semantics=sem, vmem_limit_bytes=int(vmem_mib * MIB))


def _resident(block_shape, index_map):
    return pl.BlockSpec(block_shape, index_map, pipeline_mode=pl.Buffered(1))


def _mod_spec(layer, which, ngrid):
    if ngrid == 1:
        return pl.BlockSpec((None, None, MOD_ROWS, D_MODEL), lambda i: (layer, which, 0, 0))
    return pl.BlockSpec((None, None, MOD_ROWS, D_MODEL), lambda i, j: (layer, which, 0, 0))


def _layer_vec(width, layer, ngrid):
    if ngrid == 1:
        return pl.BlockSpec((None, 1, width), lambda i: (layer, 0, 0))
    return pl.BlockSpec((None, 1, width), lambda i, j: (layer, 0, 0))


def _mod_kernel(c_ref, w_ref, b_ref, o_ref):
    c = c_ref[...]
    ca = (c * jax.nn.sigmoid(c)).astype(BF16)
    o_ref[...] = jnp.dot(ca, w_ref[...].astype(BF16), preferred_element_type=F32) + b_ref[...]


def _adaln_mod(c, ada_w, ada_b):
    depth, d, n = ada_w.shape
    bsz = c.shape[0]
    cp = jnp.zeros((MOD_ROWS, d), F32).at[:bsz].set(c)
    tn = 1024
    per_vec = d // tn
    return pl.pallas_call(
        _mod_kernel,
        out_shape=jax.ShapeDtypeStruct((depth, n // d, MOD_ROWS, d), F32),
        grid=(depth, n // tn),
        in_specs=[
            pl.BlockSpec((MOD_ROWS, d), lambda l, j: (0, 0)),
            pl.BlockSpec((None, d, tn), lambda l, j: (l, 0, j)),
            pl.BlockSpec((None, 1, tn), lambda l, j: (l, 0, j)),
        ],
        out_specs=pl.BlockSpec((None, None, MOD_ROWS, tn), lambda l, j: (l, j // per_vec, 0, j % per_vec)),
        compiler_params=_cparams(("parallel", "parallel"), 40),
        name="adaln_mod",
    )(cp, ada_w, ada_b.reshape(depth, 1, n))


def _inproj_kernel(h_ref, sc_ref, sh_ref, g_ref, w_ref, cos_ref, sin_ref,
                   u_ref, rq_ref, rk_ref, rv_ref, rg_ref, wq_ref, wk_ref, wv_ref,
                   cq_ref, ckv_ref, kr_ref, *, per_seq):
    b = pl.program_id(0) // per_seq
    x = h_ref[...]
    ms = jnp.mean(x * x, axis=-1, keepdims=True)
    y = x * lax.rsqrt(ms + EPS) * g_ref[...]
    a = (y * (1.0 + sc_ref[pl.ds(b, 1), :]) + sh_ref[pl.ds(b, 1), :]).astype(BF16)
    p = jnp.dot(a, w_ref[...], preferred_element_type=F32)
    cos = cos_ref[...]
    sin = sin_ref[...]
    u_ref[...] = p[:, _E_U:_E_RQ]
    rq_ref[...] = (p[:, _E_RQ:_E_RK] * cos + p[:, _E_RQS:_E_RKS] * sin).astype(BF16)
    rk_ref[...] = (p[:, _E_RK:_E_RV] * cos + p[:, _E_RKS:_E_END] * sin).astype(BF16)
    rv_ref[...] = p[:, _E_RV:_E_RG].astype(BF16)
    rg_ref[...] = p[:, _E_RG:_E_WQ].astype(BF16)
    wq_ref[...] = p[:, _E_WQ:_E_WK2].astype(BF16)
    wk_ref[...] = p[:, _E_WK2:_E_WV2].astype(BF16)
    wv_ref[...] = p[:, _E_WV2:_E_CQ].astype(BF16)
    cq_ref[...] = p[:, _E_CQ:_E_CKV].astype(BF16)
    ckv_ref[...] = p[:, _E_CKV:_E_KR].astype(BF16)
    kr = p[:, _E_KR:_E_KRS] * cos[:, :128] + p[:, _E_KRS:_E_RQS] * sin[:, :128]
    kr_ref[:, 0:128] = jnp.zeros_like(kr).astype(BF16)
    kr_ref[:, 128:256] = kr.astype(BF16)


def _in_projection(h2, mod, gains, w_ext, cos4, sin4, seq, layer):
    tokens, d = h2.shape
    tm = 512
    per_seq = seq // tm
    widths = (512, 256, 256, 512, 512, 512, 256, 256, 384, 128, 256)
    dtypes = (F32,) + (BF16,) * 10
    row_blk = lambda w: pl.BlockSpec((tm, w), lambda i: (i, 0))
    return pl.pallas_call(
        functools.partial(_inproj_kernel, per_seq=per_seq),
        out_shape=tuple(jax.ShapeDtypeStruct((tokens, w), t) for w, t in zip(widths, dtypes)),
        grid=(tokens // tm,),
        in_specs=[
            row_blk(d),
            _mod_spec(layer, MOD_SC1, 1),
            _mod_spec(layer, MOD_SH1, 1),
            _layer_vec(d, layer, 1),
            _resident((None, d, _E_END), lambda i: (layer, 0, 0)),
            pl.BlockSpec((tm, 256), lambda i: (i % per_seq, 0)),
            pl.BlockSpec((tm, 256), lambda i: (i % per_seq, 0)),
        ],
        out_specs=tuple(row_blk(w) for w in widths),
        compiler_params=_cparams(("parallel",), 56),
        name="in_projection",
    )(h2, mod, mod, gains, w_ext, cos4, sin4)


def _s5_prep_kernel(lr_ref, li_ref, ldt_ref, btr_ref, bti_ref, cr_ref, ci_ref,
                    a_ref, mint_ref, moutt_ref, pa_ref, pb_ref, *, chunk, nlev):
    width = chunk * S5_CH
    lr = lr_ref[...]
    li = li_ref[...]
    dt = jnp.exp(ldt_ref[...])
    btr, bti = btr_ref[...], bti_ref[...]
    cre, cim = cr_ref[...], ci_ref[...]

    def cpow(k):
        mag = jnp.exp(lr * dt * k)
        ang = li * dt * k
        return mag * jnp.cos(ang), mag * jnp.sin(ang)

    one = jnp.ones((1, 1), F32)
    ar, ai = cpow(one)
    den = lr * lr + li * li
    kr = ((ar - 1.0) * lr + ai * li) / den
    ki = (ai * lr - (ar - 1.0) * li) / den
    bbr = kr * btr - ki * bti
    bbi = kr * bti + ki * btr

    kidx = lax.broadcasted_iota(jnp.int32, (chunk, 1), 0).astype(F32)

    def outer(pr, pi, mr, mi):
        re = pr[:, None, :] * mr[None, :, :] - pi[:, None, :] * mi[None, :, :]
        im = pr[:, None, :] * mi[None, :, :] + pi[:, None, :] * mr[None, :, :]
        return re.reshape(width, S5_STATE), im.reshape(width, S5_STATE)

    er, ei = cpow((chunk - 1.0) - kidx)
    m_r, m_i = outer(er, ei, bbr, bbi)
    m_in = jnp.concatenate([m_r, m_i], axis=1)
    mint_ref[...] = m_in.T.astype(BF16)

    fr, fi = cpow(kidx + 1.0)
    w_r, w_i = outer(fr, fi, cre, cim)
    moutt_ref[...] = jnp.concatenate([w_r, -w_i], axis=1).astype(BF16)

    cc = jnp.concatenate([cre, -cim], axis=1)
    ks = lax.dot_general(cc, m_in, (((1,), (1,)), ((), ())), precision=lax.Precision.HIGHEST,
                         preferred_element_type=F32)
    lane = lax.broadcasted_iota(jnp.int32, (S5_CH, width), 1)
    for t in range(chunk):
        if t == chunk - 1:
            rows = ks
        else:
            shift = (width - S5_CH * (chunk - 1 - t)) % width
            rows = jnp.where(lane < S5_CH * (t + 1), pltpu.roll(ks, shift, axis=1), 0.0)
        a_ref[S5_CH * t:S5_CH * (t + 1), :] = rows.astype(BF16)

    lev = lax.broadcasted_iota(jnp.int32, (8, 1), 0)
    pr, pi = cpow((chunk * jnp.left_shift(1, jnp.minimum(lev, nlev))).astype(F32))
    reps = (2 * S5_STATE) // 8
    pa_ref[...] = jnp.tile(jnp.concatenate([pr, pr], axis=1), (reps, 1)).T
    pb_ref[...] = jnp.tile(jnp.concatenate([-pi, pi], axis=1), (reps, 1)).T


def _s5_prepare(lam_re, lam_im, log_dt, b_re, b_im, c_re, c_im, chunk, nlev):
    p = lam_re.shape[-1]
    h = S5_CH
    g = lam_re.shape[0] * lam_re.shape[1]
    width = chunk * h
    p2 = 2 * p
    vec = lambda: pl.BlockSpec((None, 1, p), lambda i: (i, 0, 0))
    mat = lambda: pl.BlockSpec((None, h, p), lambda i: (i, 0, 0))
    out = lambda r, c: pl.BlockSpec((None, r, c), lambda i: (i, 0, 0))
    return pl.pallas_call(
        functools.partial(_s5_prep_kernel, chunk=chunk, nlev=nlev),
        out_shape=(
            jax.ShapeDtypeStruct((g, width, width), BF16),
            jax.ShapeDtypeStruct((g, p2, width), BF16),
            jax.ShapeDtypeStruct((g, width, p2), BF16),
            jax.ShapeDtypeStruct((g, p2, p2), F32),
            jax.ShapeDtypeStruct((g, p2, p2), F32),
        ),
        grid=(g,),
        in_specs=[vec(), vec(), vec(), mat(), mat(), mat(), mat()],
        out_specs=(out(width, width), out(p2, width), out(width, p2), out(p2, p2), out(p2, p2)),
        compiler_params=_cparams(("parallel",), 32),
        name="s5_prepare",
    )(lam_re.reshape(g, 1, p), lam_im.reshape(g, 1, p),
      jnp.broadcast_to(log_dt.reshape(g, 1, 1), (g, 1, p)),
      jnp.swapaxes(b_re, -1, -2).reshape(g, h, p), jnp.swapaxes(b_im, -1, -2).reshape(g, h, p),
      c_re.reshape(g, h, p), c_im.reshape(g, h, p))


def _s5_kernel(u_ref, a_ref, mint_ref, moutt_ref, pa_ref, pb_ref, d_ref, y_ref, ut_ref, yt_ref,
               *, chunk, ncb, nlev):
    gps = S5_GROUPS_PER_STEP
    nchunks = ut_ref.shape[-1]
    for s in range(chunk):
        slab_t = u_ref[pl.ds(s, nchunks, stride=chunk), :].T
        for g in range(gps):
            ut_ref[g, S5_CH * s:S5_CH * (s + 1), :] = slab_t[S5_CH * g:S5_CH * (g + 1), :].astype(BF16)

    cidx = lax.broadcasted_iota(jnp.int32, (1, nchunks), 1) % ncb
    for g in range(gps):
        ut = ut_ref[g]
        y_intra = jnp.dot(a_ref[g], ut, preferred_element_type=F32)
        x = jnp.dot(mint_ref[g], ut, preferred_element_type=F32)
        for k in range(nlev):
            d = 1 << k
            sh = jnp.where(cidx >= d, pltpu.roll(x, d, axis=1), 0.0)
            shs = jnp.concatenate([sh[S5_STATE:], sh[:S5_STATE]], axis=0)
            x = x + pa_ref[g, :, k:k + 1] * sh + pb_ref[g, :, k:k + 1] * shs
        xp = jnp.where(cidx >= 1, pltpu.roll(x, 1, axis=1), 0.0)
        y_cross = jnp.dot(moutt_ref[g], xp.astype(BF16), preferred_element_type=F32)
        yt_ref[g] = y_intra + y_cross + d_ref[g] * ut.astype(F32)

    for s in range(chunk):
        z = jnp.concatenate([yt_ref[g, S5_CH * s:S5_CH * (s + 1), :] for g in range(gps)], axis=0)
        y_ref[pl.ds(s, nchunks, stride=chunk), :] = z.T


def _s5_mixer_pre_glu(u, ops, d_col, bsz, seq, layer):
    a_op, mint, moutt, pa, pb = ops
    chunk = S5_CHUNK
    ncb = seq // chunk
    nlev = max(1, (ncb - 1).bit_length())
    nb = min(S5_BATCH_PER_STEP, bsz)
    gps = S5_GROUPS_PER_STEP
    nsg = S5_GROUPS // gps
    width = chunk * S5_CH
    p2 = 2 * S5_STATE
    nchunks = nb * ncb
    grp = lambda r, c: pl.BlockSpec((gps, r, c), lambda sg, bp: (layer * nsg + sg, 0, 0))
    tok = pl.BlockSpec((nb * seq, LANES), lambda sg, bp: (bp, sg))
    return pl.pallas_call(
        functools.partial(_s5_kernel, chunk=chunk, ncb=ncb, nlev=nlev),
        out_shape=jax.ShapeDtypeStruct((bsz * seq, GROUP_WIDTH), F32),
        grid=(nsg, bsz // nb),
        in_specs=[tok, grp(width, width), grp(p2, width), grp(width, p2), grp(p2, p2), grp(p2, p2),
                  grp(width, 1)],
        out_specs=tok,
        scratch_shapes=[pltpu.VMEM((gps, width, nchunks), BF16), pltpu.VMEM((gps, width, nchunks), F32)],
        compiler_params=_cparams(("parallel", "parallel"), 48),
        name="s5_mixer",
    )(u, a_op, mint, moutt, pa, pb, d_col)


def _ret_kernel(q_ref, k_ref, v_ref, g_ref, dec_ref, xi_ref, zeta_ref, gc_ref, bd_ref, o_ref, st_ref):
    @pl.when(pl.program_id(1) == 0)
    def _():
        st_ref[...] = jnp.zeros_like(st_ref)

    q = q_ref[...]
    k = k_ref[...]
    v = v_ref[...]
    lane_head = lax.broadcasted_iota(jnp.int32, (1, RET_HEADS * RET_QK), 1) // RET_QK
    st = st_ref[...]
    o_cross = jnp.dot(q, st.astype(BF16), preferred_element_type=F32) * xi_ref[...]
    outs = []
    for hd in range(RET_HEADS):
        km = jnp.where(lane_head == hd, k, jnp.zeros_like(k))
        s = lax.dot_general(q, km, (((1,), (1,)), ((), ())), preferred_element_type=F32)
        p = (s * dec_ref[hd]).astype(BF16)
        vh = v[:, hd * RET_V:(hd + 1) * RET_V]
        o = jnp.dot(p, vh, preferred_element_type=F32) + o_cross[:, hd * RET_V:(hd + 1) * RET_V]
        outs.append(o * lax.rsqrt(jnp.mean(o * o, axis=-1, keepdims=True) + EPS))
    on = jnp.concatenate(outs, axis=1)
    gate = g_ref[...].astype(F32)
    o_ref[...] = (on * (gate * jax.nn.sigmoid(gate))).astype(BF16)

    kz = (k.astype(F32) * zeta_ref[...]).astype(BF16)
    kv = lax.dot_general(kz, v, (((0,), (0,)), ((), ())), preferred_element_type=F32)
    st_ref[...] = st * gc_ref[...] + kv * bd_ref[...]


def _retention(rq, rk, rv, rg, bsz, seq):
    cb = min(RET_BLOCK, seq)
    nblk = seq // cb
    hq = RET_HEADS * RET_QK
    log_gamma = jnp.log1p(-(2.0 ** (-5.0 - jnp.arange(RET_HEADS, dtype=F32))))
    idx = jnp.arange(cb, dtype=F32)
    rel = idx[:, None] - idx[None, :]
    decay = jnp.where(rel >= 0, jnp.exp(log_gamma[:, None, None] * jnp.maximum(rel, 0.0)), 0.0)
    xi = jnp.repeat(jnp.exp(log_gamma[None, :] * (idx[:, None] + 1.0)), RET_V, axis=1)
    zeta = jnp.repeat(jnp.exp(log_gamma[None, :] * (cb - 1.0 - idx[:, None])), RET_QK, axis=1)
    gchunk = jnp.repeat(jnp.exp(log_gamma * cb), RET_V)[None, :]
    blockdiag = (jnp.arange(hq)[:, None] // RET_QK == jnp.arange(GROUP_WIDTH)[None, :] // RET_V).astype(F32)
    tok = lambda w: pl.BlockSpec((cb, w), lambda b, c: (b * nblk + c, 0))
    const = lambda shape: pl.BlockSpec(shape, lambda b, c: (0,) * len(shape))
    return pl.pallas_call(
        _ret_kernel,
        out_shape=jax.ShapeDtypeStruct((bsz * seq, GROUP_WIDTH), BF16),
        grid=(bsz, nblk),
        in_specs=[tok(hq), tok(hq), tok(GROUP_WIDTH), tok(GROUP_WIDTH),
                  const((RET_HEADS, cb, cb)), const((cb, GROUP_WIDTH)), const((cb, hq)),
                  const((1, GROUP_WIDTH)), const((hq, GROUP_WIDTH))],
        out_specs=tok(GROUP_WIDTH),
        scratch_shapes=[pltpu.VMEM((hq, GROUP_WIDTH), F32)],
        compiler_params=_cparams(("parallel", "arbitrary"), 32),
        name="retention",
    )(rq, rk, rv, rg, decay, xi, zeta, gchunk, blockdiag)


def _swa_kernel(sink_ref, q_ref, kc_ref, kp_ref, vc_ref, vp_ref, o_ref, *, nsub):
    blk = pl.program_id(1)
    kall = jnp.concatenate([kp_ref[...], kc_ref[...]], axis=0)
    vall = jnp.concatenate([vp_ref[...], vc_ref[...]], axis=0)
    half = lax.broadcasted_iota(jnp.int32, (1, 2 * SWA_HD), 1) // SWA_HD
    r = lax.broadcasted_iota(jnp.int32, (WINDOW, 2 * WINDOW), 0)
    jc = lax.broadcasted_iota(jnp.int32, (WINDOW, 2 * WINDOW), 1)
    dist = r + WINDOW - jc
    band = jnp.logical_and(dist >= 0, dist < WINDOW)
    first_key = jnp.where(blk > 0, 0, WINDOW)
    band_first = jnp.logical_and(band, jc >= first_key)
    for sb in range(nsub):
        valid = band_first if sb == 0 else band
        r0 = sb * WINDOW
        for j in range(SWA_KV_HEADS):
            kk = kall[r0:r0 + 2 * WINDOW, 128 * j:128 * (j + 1)]
            vv = vall[r0:r0 + 2 * WINDOW, 128 * j:128 * (j + 1)]
            zk = jnp.zeros_like(kk)
            kcat = jnp.concatenate([jnp.where(half == 0, kk, zk), jnp.where(half == 1, kk, zk)], axis=0)
            vcat = jnp.concatenate([jnp.where(half == 0, vv, zk), jnp.where(half == 1, vv, zk)], axis=0)
            q2 = jnp.concatenate([q_ref[r0:r0 + WINDOW, 256 * j:256 * j + 128],
                                  q_ref[r0:r0 + WINDOW, 256 * j + 128:256 * j + 256]], axis=0)
            s_all = lax.dot_general(q2, kcat, (((1,), (1,)), ((), ())), preferred_element_type=F32)
            p_rows, inv_rows = [], []
            for a in range(2):
                p_cols, invs = [], []
                for e in range(2):
                    sink = sink_ref[4 * j + 2 * a + e]
                    s = s_all[WINDOW * a:WINDOW * (a + 1), 2 * WINDOW * e:2 * WINDOW * (e + 1)]
                    s = jnp.where(valid, s, NEG)
                    m = jnp.maximum(jnp.max(s, axis=-1, keepdims=True), sink)
                    p = jnp.exp(s - m)
                    denom = jnp.sum(p, axis=-1, keepdims=True) + jnp.exp(sink - m)
                    p_cols.append(p.astype(BF16))
                    invs.append(1.0 / denom)
                p_rows.append(jnp.concatenate(p_cols, axis=1))
                inv_rows.append(jnp.where(half == 0, invs[0], invs[1]))
            p_all = jnp.concatenate(p_rows, axis=0)
            o = jnp.dot(p_all, vcat, preferred_element_type=F32)
            for a in range(2):
                o_ref[r0:r0 + WINDOW, 256 * j + 128 * a:256 * j + 128 * (a + 1)] = (
                    o[WINDOW * a:WINDOW * (a + 1)] * inv_rows[a]).astype(BF16)


def _swa(wq, wk2, wv2, sinks, bsz, seq, layer):
    qb = min(SWA_BLOCK, seq)
    nblk = seq // qb
    per = qb // WINDOW
    cur = lambda w: pl.BlockSpec((qb, w), lambda b, i: (b * nblk + i, 0))
    prev = lambda w: pl.BlockSpec((WINDOW, w), lambda b, i: (jnp.maximum((b * nblk + i) * per - 1, 0), 0))
    return pl.pallas_call(
        functools.partial(_swa_kernel, nsub=per),
        out_shape=jax.ShapeDtypeStruct((bsz * seq, GROUP_WIDTH), BF16),
        grid=(bsz, nblk),
        in_specs=[pl.BlockSpec(memory_space=pltpu.SMEM),
                  cur(GROUP_WIDTH), cur(256), prev(256), cur(256), prev(256)],
        out_specs=cur(GROUP_WIDTH),
        compiler_params=_cparams(("parallel", "parallel"), 32),
        name="swa",
    )(sinks[layer], wq, wk2, wk2, wv2, wv2)


def _mla_up_kernel(cq_ref, ckv_ref, kr_ref, qn_ref, kvn_ref, wq_ref, wkv_ref, cos_ref, sin_ref,
                   q_ref, k_ref, vt_ref):
    cq = cq_ref[...].astype(F32)
    nq = (cq * lax.rsqrt(jnp.mean(cq * cq, axis=-1, keepdims=True) + EPS) * qn_ref[...]).astype(BF16)
    qq = jnp.dot(nq, wq_ref[...], preferred_element_type=F32)
    ckv = ckv_ref[...].astype(F32)
    nkv = (ckv * lax.rsqrt(jnp.mean(ckv * ckv, axis=-1, keepdims=True) + EPS) * kvn_ref[...]).astype(BF16)
    kv = jnp.dot(nkv, wkv_ref[...], preferred_element_type=F32)
    cos, sin = cos_ref[...], sin_ref[...]
    kr = kr_ref[...].astype(F32)
    hw = MLA_QK_PAD
    for hd in range(MLA_HEADS):
        q_ref[:, hw * hd:hw * (hd + 1)] = (
            qq[:, hw * hd:hw * (hd + 1)] * cos
            + qq[:, hw * (MLA_HEADS + hd):hw * (MLA_HEADS + hd + 1)] * sin).astype(BF16)
        k_ref[:, hw * hd:hw * (hd + 1)] = (kv[:, hw * hd:hw * (hd + 1)] + kr).astype(BF16)
    vt_ref[...] = kv[:, hw * MLA_HEADS:].T.astype(BF16)


def _mla_up(cq, ckv, kr_pad, q_norm, kv_norm, wq_ext, wkv_ext, cosq, sinq, seq, layer):
    tokens = cq.shape[0]
    tm = min(MLA_TK, seq)
    per_seq = seq // tm
    hw = MLA_QK_PAD
    row = lambda w: pl.BlockSpec((tm, w), lambda i: (i, 0))
    tab = lambda: pl.BlockSpec((tm, hw), lambda i: (i % per_seq, 0))
    lay = lambda a: pl.BlockSpec((None,) + a.shape[1:], lambda i: (layer, 0, 0))
    return pl.pallas_call(
        _mla_up_kernel,
        out_shape=(jax.ShapeDtypeStruct((tokens, MLA_HEADS * hw), BF16),
                   jax.ShapeDtypeStruct((tokens, MLA_HEADS * hw), BF16),
                   jax.ShapeDtypeStruct((tokens // tm, MLA_HEADS * MLA_V, tm), BF16)),
        grid=(tokens // tm,),
        in_specs=[row(MLA_Q_RANK), row(MLA_KV_RANK), row(hw), lay(q_norm), lay(kv_norm),
                  lay(wq_ext), lay(wkv_ext), tab(), tab()],
        out_specs=(row(MLA_HEADS * hw), row(MLA_HEADS * hw),
                   pl.BlockSpec((None, MLA_HEADS * MLA_V, tm), lambda i: (i, 0, 0))),
        compiler_params=_cparams(("parallel",), 40),
        name="mla_up",
    )(cq, ckv, kr_pad, q_norm, kv_norm, wq_ext, wkv_ext, cosq, sinq)


def _mla_attn_kernel(q_ref, k_ref, vt_ref, o_ref, m_ref, l_ref, acc_ref, *, tq, tk, heads):
    qi = pl.program_id(2)
    hw = MLA_QK_PAD
    m_ref[...] = jnp.full_like(m_ref, NEG)
    l_ref[...] = jnp.zeros_like(l_ref)
    acc_ref[...] = jnp.zeros_like(acc_ref)

    def step(kb, masked):
        row0 = pl.multiple_of(kb * tk, tk)
        sts, ps, alphas = [], [], []
        for hd in range(heads):
            q = q_ref[:, hw * hd:hw * (hd + 1)]
            ks = k_ref[pl.ds(row0, tk), hw * hd:hw * (hd + 1)]
            st = lax.dot_general(ks, q, (((1,), (1,)), ((), ())), preferred_element_type=F32)
            if masked:
                key = lax.broadcasted_iota(jnp.int32, (tk, tq), 0)
                qry = lax.broadcasted_iota(jnp.int32, (tk, tq), 1)
                st = jnp.where(key <= qry, st, NEG)
            sts.append(st)
        for hd in range(heads):
            m_old = m_ref[hd]
            m_new = jnp.maximum(m_old, jnp.max(sts[hd], axis=0, keepdims=True))
            alpha = jnp.exp2(m_old - m_new)
            p = jnp.exp2(sts[hd] - m_new)
            l_ref[hd] = alpha * l_ref[hd] + jnp.sum(p, axis=0, keepdims=True)
            m_ref[hd] = m_new
            ps.append(p.astype(BF16))
            alphas.append(alpha)
        for hd in range(heads):
            vt = vt_ref[kb, MLA_V * hd:MLA_V * (hd + 1), :]
            acc_ref[hd] = alphas[hd] * acc_ref[hd] + jnp.dot(vt, ps[hd], preferred_element_type=F32)

    def body(kb, carry):
        step(kb, False)
        return carry

    lax.fori_loop(0, qi, body, 0)
    step(qi, True)
    for hd in range(heads):
        o_ref[:, MLA_V * hd:MLA_V * (hd + 1)] = (acc_ref[hd] / l_ref[hd]).T.astype(BF16)


def _mla_attention(q, k, vt, bsz, seq):
    tq = tk = min(MLA_TQ, seq)
    nq = seq // tq
    hp = MLA_HEADS_PER_STEP
    hw = MLA_QK_PAD
    return pl.pallas_call(
        functools.partial(_mla_attn_kernel, tq=tq, tk=tk, heads=hp),
        out_shape=jax.ShapeDtypeStruct((bsz * seq, MLA_HEADS * MLA_V), BF16),
        grid=(bsz, MLA_HEADS // hp, nq),
        in_specs=[pl.BlockSpec((tq, hp * hw), lambda b, h, i: (b * nq + i, h)),
                  pl.BlockSpec((seq, hp * hw), lambda b, h, i: (b, h)),
                  pl.BlockSpec((nq, hp * MLA_V, tk), lambda b, h, i: (b, h, 0))],
        out_specs=pl.BlockSpec((tq, hp * MLA_V), lambda b, h, i: (b * nq + i, h)),
        scratch_shapes=[pltpu.VMEM((hp, 1, tq), F32), pltpu.VMEM((hp, 1, tq), F32),
                        pltpu.VMEM((hp, MLA_V, tq), F32)],
        compiler_params=_cparams(("parallel", "parallel", "arbitrary"), 48),
        name="mla_attention",
    )(q, k, vt)


def _outproj_kernel(ys5_ref, yret_ref, yswa_ref, ymla_ref, h_ref, gt_ref, gw_ref, gb_ref, w_ref, o_ref,
                    *, per_seq):
    b = pl.program_id(0) // per_seq
    y = ys5_ref[...]
    z = jax.nn.gelu(y, approximate=True)
    gl = jnp.dot(z.astype(BF16), gw_ref[...], preferred_element_type=F32) + gb_ref[...]
    s5 = (z * jax.nn.sigmoid(gl)).astype(BF16)
    gw = GROUP_WIDTH
    mixed = jnp.dot(s5, w_ref[0:gw, :], preferred_element_type=F32)
    mixed += jnp.dot(yret_ref[...], w_ref[gw:2 * gw, :], preferred_element_type=F32)
    mixed += jnp.dot(yswa_ref[...], w_ref[2 * gw:3 * gw, :], preferred_element_type=F32)
    mixed += jnp.dot(ymla_ref[...], w_ref[3 * gw:4 * gw, :], preferred_element_type=F32)
    o_ref[...] = h_ref[...] + gt_ref[pl.ds(b, 1), :] * mixed


def _out_projection(ys5, yret, yswa, ymla, h2, mod, glu_w, glu_b, w_out, seq, layer):
    tokens, d = h2.shape
    tm = min(512, seq)
    per_seq = seq // tm
    gw = GROUP_WIDTH
    row = lambda w: pl.BlockSpec((tm, w), lambda i: (i, 0))
    return pl.pallas_call(
        functools.partial(_outproj_kernel, per_seq=per_seq),
        out_shape=jax.ShapeDtypeStruct((tokens, d), F32),
        grid=(tokens // tm,),
        in_specs=[row(gw), row(gw), row(gw), row(gw), row(d),
                  _mod_spec(layer, MOD_GT1, 1),
                  pl.BlockSpec((None, gw, gw), lambda i: (layer, 0, 0)),
                  _layer_vec(gw, layer, 1),
                  _resident((None, 4 * gw, d), lambda i: (layer, 0, 0))],
        out_specs=row(d),
        compiler_params=_cparams(("parallel",), 48),
        name="out_projection",
    )(ys5, yret, yswa, ymla, h2, mod, glu_w, glu_b, w_out)


def _mlp_kernel(h_ref, sc_ref, sh_ref, gt_ref, g_ref, fg_ref, w1_ref, w2_ref, o_ref, a_ref, *, per_seq, final):
    j = pl.program_id(1)
    b = pl.program_id(0) // per_seq

    @pl.when(j == 0)
    def _():
        x = h_ref[...]
        ms = jnp.mean(x * x, axis=-1, keepdims=True)
        y = x * lax.rsqrt(ms + EPS) * g_ref[...]
        a_ref[...] = (y * (1.0 + sc_ref[pl.ds(b, 1), :]) + sh_ref[pl.ds(b, 1), :]).astype(BF16)
        o_ref[...] = jnp.zeros_like(o_ref)

    hid = jnp.dot(a_ref[...], w1_ref[...], preferred_element_type=F32)
    hid = jnp.square(jnp.maximum(hid, 0.0)).astype(BF16)
    o_ref[...] += jnp.dot(hid, w2_ref[...], preferred_element_type=F32)

    @pl.when(j == pl.num_programs(1) - 1)
    def _():
        out = h_ref[...] + gt_ref[pl.ds(b, 1), :] * o_ref[...]
        if final:
            ms = jnp.mean(out * out, axis=-1, keepdims=True)
            out = out * lax.rsqrt(ms + EPS) * fg_ref[...]
        o_ref[...] = out


def _mlp(h2, mod, gains, final_gain, w1, w2, seq, layer, final):
    tokens, d = h2.shape
    dff = w1.shape[-1]
    tm = min(512, seq)
    tf = 1024
    per_seq = seq // tm
    return pl.pallas_call(
        functools.partial(_mlp_kernel, per_seq=per_seq, final=final),
        out_shape=jax.ShapeDtypeStruct((tokens, d), F32),
        grid=(tokens // tm, dff // tf),
        in_specs=[pl.BlockSpec((tm, d), lambda i, j: (i, 0)),
                  _mod_spec(layer, MOD_SC2, 2), _mod_spec(layer, MOD_SH2, 2), _mod_spec(layer, MOD_GT2, 2),
                  _layer_vec(d, layer, 2),
                  pl.BlockSpec((1, d), lambda i, j: (0, 0)),
                  pl.BlockSpec((None, d, tf), lambda i, j: (layer, 0, j)),
                  pl.BlockSpec((None, tf, d), lambda i, j: (layer, j, 0))],
        out_specs=pl.BlockSpec((tm, d), lambda i, j: (i, 0)),
        scratch_shapes=[pltpu.VMEM((tm, d), BF16)],
        compiler_params=_cparams(("parallel", "arbitrary"), 48),
        name="mlp",
    )(h2, mod, mod, mod, gains, final_gain, w1, w2)


def _swap_halves(w, heads, hd):
    lead = w.shape[:-1]
    w4 = w.reshape(lead + (heads, 2, hd // 2))
    return jnp.concatenate([w4[..., 1:, :], w4[..., :1, :]], axis=-2).reshape(lead + (heads * hd,))


def _in_weight_ext(w_in):
    col = lambda a, b: w_in[..., a:b]
    z64 = jnp.zeros(w_in.shape[:-1] + (64,), w_in.dtype)
    k_scale = RET_QK ** -0.5
    kr = col(_O_KR, _O_END)
    parts = [
        col(_O_U, _O_RK),
        col(_O_RK, _O_RV) * k_scale,
        col(_O_RV, _O_WQ),
        col(_O_WQ, _O_WK) * (SWA_HD ** -0.5),
        col(_O_WK, _O_WK + 64), col(_O_WK, _O_WK + 64), col(_O_WK + 64, _O_WV), col(_O_WK + 64, _O_WV),
        col(_O_WV, _O_WV + 64), col(_O_WV, _O_WV + 64), col(_O_WV + 64, _O_CQ), col(_O_WV + 64, _O_CQ),
        col(_O_CQ, _O_KR),
        kr, z64,
        _swap_halves(kr, 1, MLA_ROPE), z64,
        _swap_halves(col(_O_RQ, _O_RK), RET_HEADS, RET_QK),
        _swap_halves(col(_O_RK, _O_RV), RET_HEADS, RET_QK) * k_scale,
    ]
    w_ext = jnp.concatenate(parts, axis=-1).astype(BF16)
    assert w_ext.shape[-1] == _E_END
    return w_ext


def _mla_weight_ext(w_uq, w_ukv):
    hq = MLA_NOPE + MLA_ROPE
    hk = MLA_NOPE + MLA_V
    scale = hq ** -0.5 * math.log2(math.e)
    zq = lambda n: jnp.zeros(w_uq.shape[:-1] + (n,), w_uq.dtype)
    zk128 = jnp.zeros(w_ukv.shape[:-1] + (128,), w_ukv.dtype)
    plain, swapped, kcols, vcols = [], [], [], []
    for h in range(MLA_HEADS):
        plain += [w_uq[..., h * hq:(h + 1) * hq], zq(64)]
        swapped += [zq(128), _swap_halves(w_uq[..., h * hq + MLA_NOPE:(h + 1) * hq], 1, MLA_ROPE), zq(64)]
        kcols += [w_ukv[..., h * hk:h * hk + MLA_NOPE], zk128]
        vcols += [w_ukv[..., h * hk + MLA_NOPE:(h + 1) * hk]]
    wq_ext = (jnp.concatenate(plain + swapped, axis=-1) * scale).astype(BF16)
    wkv_ext = jnp.concatenate(kcols + vcols, axis=-1).astype(BF16)
    return wq_ext, wkv_ext


def _rotary_tables(seq):
    d = RET_QK
    inv = ROPE_BASE ** (-jnp.arange(0, d, 2, dtype=F32) / d)
    ang = jnp.arange(seq, dtype=F32)[:, None] * inv[None, :]
    cos, sin = jnp.cos(ang), jnp.sin(ang)
    cos1 = jnp.concatenate([cos, cos], axis=1)
    sin1 = jnp.concatenate([-sin, sin], axis=1)
    cos4, sin4 = jnp.tile(cos1, (1, 4)), jnp.tile(sin1, (1, 4))
    zeros64 = jnp.zeros((seq, 64), F32)
    cosq = jnp.concatenate([jnp.ones((seq, MLA_NOPE), F32), cos1, zeros64], axis=1)
    sinq = jnp.concatenate([jnp.zeros((seq, MLA_NOPE), F32), sin1, zeros64], axis=1)
    return cos4, sin4, cosq, sinq


def kernel(x, c, norm1_g, norm2_g, ada_w, ada_b, w_in, s5_lambda_re, s5_lambda_im, s5_log_dt, s5_b_re, s5_b_im, s5_c_re, s5_c_im, s5_d, s5_glu_w, s5_glu_b, swa_sinks, mla_q_norm, mla_kv_norm, mla_w_uq, mla_w_ukv, w_out, mlp_w1, mlp_w2, final_norm_g):
    bsz, seq, d = x.shape
    depth = ada_w.shape[0]
    tokens = bsz * seq
    h = x.reshape(tokens, d)
    mod = _adaln_mod(c, ada_w, ada_b)
    cos4, sin4, cosq, sinq = _rotary_tables(seq)
    w_ext = _in_weight_ext(w_in)
    wq_ext, wkv_ext = _mla_weight_ext(mla_w_uq, mla_w_ukv)
    w_out_b, glu_w_b = w_out.astype(BF16), s5_glu_w.astype(BF16)
    w1_b, w2_b = mlp_w1.astype(BF16), mlp_w2.astype(BF16)
    norm1 = norm1_g.reshape(depth, 1, d)
    norm2 = norm2_g.reshape(depth, 1, d)
    glu_b = s5_glu_b.reshape(depth, 1, -1)
    q_norm = mla_q_norm.reshape(depth, 1, -1)
    kv_norm = mla_kv_norm.reshape(depth, 1, -1)
    ncb = seq // S5_CHUNK
    s5_ops = _s5_prepare(s5_lambda_re, s5_lambda_im, s5_log_dt, s5_b_re, s5_b_im, s5_c_re, s5_c_im,
                         S5_CHUNK, max(1, (ncb - 1).bit_length()))
    d_col = jnp.tile(s5_d.reshape(depth * S5_GROUPS, S5_CH), (1, S5_CHUNK))[:, :, None]
    for l in range(depth):
        (u, rq, rk, rv, rg, wq, wk2, wv2, cq, ckv, kr_pad) = _in_projection(
            h, mod, norm1, w_ext, cos4, sin4, seq, l)
        y_s5 = _s5_mixer_pre_glu(u, s5_ops, d_col, bsz, seq, l)
        y_ret = _retention(rq, rk, rv, rg, bsz, seq)
        y_swa = _swa(wq, wk2, wv2, swa_sinks, bsz, seq, l)
        mq, mk, mvt = _mla_up(cq, ckv, kr_pad, q_norm, kv_norm, wq_ext, wkv_ext, cosq, sinq, seq, l)
        y_mla = _mla_attention(mq, mk, mvt, bsz, seq)
        h = _out_projection(y_s5, y_ret, y_swa, y_mla, h, mod, glu_w_b, glu_b, w_out_b, seq, l)
        h = _mlp(h, mod, norm2, final_norm_g.reshape(1, d), w1_b, w2_b, seq, l, final=(l == depth - 1))
    return h.reshape(bsz, seq, d)
```

```python
import functools
import math

import numpy as np
import jax
import jax.numpy as jnp
from jax import lax
from jax.experimental import pallas as pl
from jax.experimental.pallas import tpu as pltpu

F32 = jnp.float32
BF16 = jnp.bfloat16

D_MODEL = 2048
GROUP_WIDTH = 512
S5_CH = 16
S5_GROUPS = GROUP_WIDTH // S5_CH
S5_STATE = 64
RET_HEADS = 4
RET_QK = 64
RET_V = 128
SWA_HD = 64
SWA_HEADS = 8
SWA_KV_HEADS = 2
WINDOW = 128
MLA_HEADS = 4
MLA_Q_RANK = 384
MLA_KV_RANK = 128
MLA_NOPE = 128
MLA_ROPE = 64
MLA_V = 128
MLA_QK_PAD = 256
ROPE_BASE = 10000.0
EPS = 1e-6
NEG = -1e30

MIB = 1024 * 1024
LANES = 128
MOD_ROWS = 8
S5_CHUNK = 32
S5_GROUPS_PER_STEP = LANES // S5_CH
S5_BATCH_PER_STEP = 2
RET_BLOCK = 512
SWA_BLOCK = 512
MLA_TQ = 512
MLA_TK = 512
MLA_HEADS_PER_STEP = 4
MOD_SH1, MOD_SC1, MOD_GT1, MOD_SH2, MOD_SC2, MOD_GT2 = range(6)

_O_U, _O_RQ, _O_RK, _O_RV, _O_RG, _O_WQ, _O_WK, _O_WV, _O_CQ, _O_CKV, _O_KR, _O_END = (
    0, 512, 768, 1024, 1536, 2048, 2560, 2688, 2816, 3200, 3328, 3392)


def _cparams(sem, vmem_mib):
    return pltpu.CompilerParams(dimension_semantics=sem, vmem_limit_bytes=int(vmem_mib * MIB))


def _resident(block_shape, index_map):
    return pl.BlockSpec(block_shape, index_map, pipeline_mode=pl.Buffered(1))


def _mod_spec(layer, which, ngrid):
    if ngrid == 1:
        return pl.BlockSpec((None, None, MOD_ROWS, D_MODEL), lambda i: (layer, which, 0, 0))
    return pl.BlockSpec((None, None, MOD_ROWS, D_MODEL), lambda i, j: (layer, which, 0, 0))


def _layer_vec(width, layer, ngrid):
    if ngrid == 1:
        return pl.BlockSpec((None, 1, width), lambda i: (layer, 0, 0))
    return pl.BlockSpec((None, 1, width), lambda i, j: (layer, 0, 0))


def _mod_kernel(c_ref, w_ref, b_ref, o_ref):
    c = c_ref[...]
    ca = (c * jax.nn.sigmoid(c)).astype(BF16)
    o_ref[...] = jnp.dot(ca, w_ref[...].astype(BF16), preferred_element_type=F32) + b_ref[...]


def _adaln_mod(c, ada_w, ada_b):
    depth, d, n = ada_w.shape
    bsz = c.shape[0]
    cp = jnp.zeros((MOD_ROWS, d), F32).at[:bsz].set(c)
    tn = 1024
    per_vec = d // tn
    return pl.pallas_call(
        _mod_kernel,
        out_shape=jax.ShapeDtypeStruct((depth, n // d, MOD_ROWS, d), F32),
        grid=(depth, n // tn),
        in_specs=[
            pl.BlockSpec((MOD_ROWS, d), lambda l, j: (0, 0)),
            pl.BlockSpec((None, d, tn), lambda l, j: (l, 0, j)),
            pl.BlockSpec((None, 1, tn), lambda l, j: (l, 0, j)),
        ],
        out_specs=pl.BlockSpec((None, None, MOD_ROWS, tn), lambda l, j: (l, j // per_vec, 0, j % per_vec)),
        compiler_params=_cparams(("parallel", "parallel"), 40),
        name="adaln_mod",
    )(cp, ada_w, ada_b.reshape(depth, 1, n))


def _inproj_kernel(h_ref, sc_ref, sh_ref, g_ref, w_ref, cos_ref, sin_ref, swap_ref, krsel_ref, dup_ref,
                   u_ref, rq_ref, rk_ref, rv_ref, rg_ref, wq_ref, wk_ref, wv_ref,
                   cq_ref, ckv_ref, kr_ref, *, per_seq):
    b = pl.program_id(0) // per_seq
    x = h_ref[...]
    ms = jnp.mean(x * x, axis=-1, keepdims=True)
    y = x * lax.rsqrt(ms + EPS) * g_ref[...]
    a = (y * (1.0 + sc_ref[pl.ds(b, 1), :]) + sh_ref[pl.ds(b, 1), :]).astype(BF16)
    p = jnp.dot(a, w_ref[...], preferred_element_type=F32)
    cos = cos_ref[...]
    sin = sin_ref[...]
    swap = swap_ref[...]

    def rope(v):
        return v * cos + jnp.dot(v.astype(BF16), swap, preferred_element_type=F32) * sin

    u_ref[...] = p[:, _O_U:_O_RQ]
    rq_ref[...] = rope(p[:, _O_RQ:_O_RK]).astype(BF16)
    rk_ref[...] = rope(p[:, _O_RK:_O_RV] * (RET_QK ** -0.5)).astype(BF16)
    rv_ref[...] = p[:, _O_RV:_O_RG].astype(BF16)
    rg_ref[...] = p[:, _O_RG:_O_WQ].astype(BF16)
    wq_ref[...] = (p[:, _O_WQ:_O_WK] * (SWA_HD ** -0.5)).astype(BF16)
    dup = dup_ref[...]
    wk_ref[...] = jnp.dot(p[:, _O_WK:_O_WV].astype(BF16), dup, preferred_element_type=F32).astype(BF16)
    wv_ref[...] = jnp.dot(p[:, _O_WV:_O_CQ].astype(BF16), dup, preferred_element_type=F32).astype(BF16)
    cq_ref[...] = p[:, _O_CQ:_O_CKV].astype(BF16)
    ckv_ref[...] = p[:, _O_CKV:_O_KR].astype(BF16)
    kr2 = jnp.dot(p[:, _O_KR:_O_END].astype(BF16), krsel_ref[...], preferred_element_type=F32)
    kr = kr2[:, :128] * cos[:, :128] + kr2[:, 128:] * sin[:, :128]
    kr_ref[:, 0:128] = jnp.zeros_like(kr).astype(BF16)
    kr_ref[:, 128:256] = kr.astype(BF16)


def _lane_maps():
    half = RET_QK // 2
    j = np.arange(RET_HEADS * RET_QK)
    swap = np.zeros((j.size, j.size), np.float32)
    swap[(j // RET_QK) * RET_QK + (j % RET_QK + half) % RET_QK, j] = 1.0
    i = np.arange(MLA_ROPE)
    krsel = np.zeros((MLA_ROPE, 256), np.float32)
    krsel[i, i] = 1.0
    krsel[(i + half) % MLA_ROPE, 128 + i] = 1.0
    c = np.arange(256)
    dup = np.zeros((128, 256), np.float32)
    dup[(c // 128) * SWA_HD + c % SWA_HD, c] = 1.0
    return jnp.asarray(swap, BF16), jnp.asarray(krsel, BF16), jnp.asarray(dup, BF16)


def _in_projection(h2, mod, gains, w_in_b, cos4, sin4, seq, layer):
    tokens, d = h2.shape
    n_in = w_in_b.shape[-1]
    tm = 512
    per_seq = seq // tm
    widths = (512, 256, 256, 512, 512, 512, 256, 256, 384, 128, 256)
    dtypes = (F32,) + (BF16,) * 10
    row_blk = lambda w: pl.BlockSpec((tm, w), lambda i: (i, 0))
    const = lambda a: pl.BlockSpec(a.shape, lambda i: (0, 0))
    swap, krsel, dup = _lane_maps()
    return pl.pallas_call(
        functools.partial(_inproj_kernel, per_seq=per_seq),
        out_shape=tuple(jax.ShapeDtypeStruct((tokens, w), t) for w, t in zip(widths, dtypes)),
        grid=(tokens // tm,),
        in_specs=[
            row_blk(d),
            _mod_spec(layer, MOD_SC1, 1),
            _mod_spec(layer, MOD_SH1, 1),
            _layer_vec(d, layer, 1),
            _resident((None, d, n_in), lambda i: (layer, 0, 0)),
            pl.BlockSpec((tm, 256), lambda i: (i % per_seq, 0)),
            pl.BlockSpec((tm, 256), lambda i: (i % per_seq, 0)),
            const(swap), const(krsel), const(dup),
        ],
        out_specs=tuple(row_blk(w) for w in widths),
        compiler_params=_cparams(("parallel",), 56),
        name="in_projection",
    )(h2, mod, mod, gains, w_in_b, cos4, sin4, swap, krsel, dup)


def _s5_prep_kernel(lr_ref, li_ref, ldt_ref, btr_ref, bti_ref, cr_ref, ci_ref, dlag0_ref,
                    a_ref, mint_ref, moutt_ref, pa_ref, pb_ref, *, chunk, nlev):
    width = chunk * S5_CH
    lr = lr_ref[...]
    li = li_ref[...]
    dt = jnp.exp(ldt_ref[...])
    btr, bti = btr_ref[...], bti_ref[...]
    cre, cim = cr_ref[...], ci_ref[...]

    def cpow(k):
        mag = jnp.exp(lr * dt * k)
        ang = li * dt * k
        return mag * jnp.cos(ang), mag * jnp.sin(ang)

    one = jnp.ones((1, 1), F32)
    ar, ai = cpow(one)
    den = lr * lr + li * li
    kr = ((ar - 1.0) * lr + ai * li) / den
    ki = (ai * lr - (ar - 1.0) * li) / den
    bbr = kr * btr - ki * bti
    bbi = kr * bti + ki * btr

    kidx = lax.broadcasted_iota(jnp.int32, (chunk, 1), 0).astype(F32)

    def outer(pr, pi, mr, mi):
        re = pr[:, None, :] * mr[None, :, :] - pi[:, None, :] * mi[None, :, :]
        im = pr[:, None, :] * mi[None, :, :] + pi[:, None, :] * mr[None, :, :]
        return re.reshape(width, S5_STATE), im.reshape(width, S5_STATE)

    er, ei = cpow((chunk - 1.0) - kidx)
    m_r, m_i = outer(er, ei, bbr, bbi)
    m_in = jnp.concatenate([m_r, m_i], axis=1)
    mint_ref[...] = m_in.T.astype(BF16)

    fr, fi = cpow(kidx + 1.0)
    w_r, w_i = outer(fr, fi, cre, cim)
    moutt_ref[...] = jnp.concatenate([w_r, -w_i], axis=1).astype(BF16)

    cc = jnp.concatenate([cre, -cim], axis=1)
    ks = lax.dot_general(cc, m_in, (((1,), (1,)), ((), ())), precision=lax.Precision.HIGHEST,
                         preferred_element_type=F32)
    ks = ks + dlag0_ref[...]
    lane = lax.broadcasted_iota(jnp.int32, (S5_CH, width), 1)
    for t in range(chunk):
        if t == chunk - 1:
            rows = ks
        else:
            shift = (width - S5_CH * (chunk - 1 - t)) % width
            rows = jnp.where(lane < S5_CH * (t + 1), pltpu.roll(ks, shift, axis=1), 0.0)
        a_ref[S5_CH * t:S5_CH * (t + 1), :] = rows.astype(BF16)

    lev = lax.broadcasted_iota(jnp.int32, (8, 1), 0)
    pr, pi = cpow((chunk * jnp.left_shift(1, jnp.minimum(lev, nlev))).astype(F32))
    reps = (2 * S5_STATE) // 8
    pa_ref[...] = jnp.tile(jnp.concatenate([pr, pr], axis=1), (reps, 1)).T
    pb_ref[...] = jnp.tile(jnp.concatenate([-pi, pi], axis=1), (reps, 1)).T


def _s5_prepare(lam_re, lam_im, log_dt, b_re, b_im, c_re, c_im, d_skip, chunk, nlev):
    p = lam_re.shape[-1]
    h = S5_CH
    g = lam_re.shape[0] * lam_re.shape[1]
    width = chunk * h
    d_lag0 = jnp.concatenate([jnp.zeros((g, h, width - h), F32),
                              d_skip.reshape(g, h, 1) * jnp.eye(h, dtype=F32)], axis=-1)
    p2 = 2 * p
    vec = lambda: pl.BlockSpec((None, 1, p), lambda i: (i, 0, 0))
    mat = lambda: pl.BlockSpec((None, h, p), lambda i: (i, 0, 0))
    out = lambda r, c: pl.BlockSpec((None, r, c), lambda i: (i, 0, 0))
    return pl.pallas_call(
        functools.partial(_s5_prep_kernel, chunk=chunk, nlev=nlev),
        out_shape=(
            jax.ShapeDtypeStruct((g, width, width), BF16),
            jax.ShapeDtypeStruct((g, p2, width), BF16),
            jax.ShapeDtypeStruct((g, width, p2), BF16),
            jax.ShapeDtypeStruct((g, p2, p2), F32),
            jax.ShapeDtypeStruct((g, p2, p2), F32),
        ),
        grid=(g,),
        in_specs=[vec(), vec(), vec(), mat(), mat(), mat(), mat(),
                  pl.BlockSpec((None, h, width), lambda i: (i, 0, 0))],
        out_specs=(out(width, width), out(p2, width), out(width, p2), out(p2, p2), out(p2, p2)),
        compiler_params=_cparams(("parallel",), 32),
        name="s5_prepare",
    )(lam_re.reshape(g, 1, p), lam_im.reshape(g, 1, p),
      jnp.broadcast_to(log_dt.reshape(g, 1, 1), (g, 1, p)),
      jnp.swapaxes(b_re, -1, -2).reshape(g, h, p), jnp.swapaxes(b_im, -1, -2).reshape(g, h, p),
      c_re.reshape(g, h, p), c_im.reshape(g, h, p), d_lag0)


def _s5_kernel(u_ref, a_ref, mint_ref, moutt_ref, pa_ref, pb_ref, y_ref, ut_ref, yt_ref,
               *, chunk, ncb, nlev):
    gps = S5_GROUPS_PER_STEP
    nchunks = ut_ref.shape[-1]
    for s in range(chunk):
        slab_t = u_ref[pl.ds(s, nchunks, stride=chunk), :].T
        for g in range(gps):
            ut_ref[g, S5_CH * s:S5_CH * (s + 1), :] = slab_t[S5_CH * g:S5_CH * (g + 1), :].astype(BF16)

    cidx = lax.broadcasted_iota(jnp.int32, (1, nchunks), 1) % ncb
    for g in range(gps):
        ut = ut_ref[g]
        y_intra = jnp.dot(a_ref[g], ut, preferred_element_type=F32)
        x = jnp.dot(mint_ref[g], ut, preferred_element_type=F32)
        for k in range(nlev):
            d = 1 << k
            sh = jnp.where(cidx >= d, pltpu.roll(x, d, axis=1), 0.0)
            shs = jnp.concatenate([sh[S5_STATE:], sh[:S5_STATE]], axis=0)
            x = x + pa_ref[g, :, k:k + 1] * sh + pb_ref[g, :, k:k + 1] * shs
        xp = jnp.where(cidx >= 1, pltpu.roll(x, 1, axis=1), 0.0)
        y_cross = jnp.dot(moutt_ref[g], xp.astype(BF16), preferred_element_type=F32)
        yt_ref[g] = y_intra + y_cross

    for s in range(chunk):
        z = jnp.concatenate([yt_ref[g, S5_CH * s:S5_CH * (s + 1), :] for g in range(gps)], axis=0)
        y_ref[pl.ds(s, nchunks, stride=chunk), :] = z.T


def _s5_mixer_pre_glu(u, ops, bsz, seq, layer):
    a_op, mint, moutt, pa, pb = ops
    chunk = S5_CHUNK
    ncb = seq // chunk
    nlev = max(1, (ncb - 1).bit_length())
    nb = min(S5_BATCH_PER_STEP, bsz)
    gps = S5_GROUPS_PER_STEP
    nsg = S5_GROUPS // gps
    width = chunk * S5_CH
    p2 = 2 * S5_STATE
    nchunks = nb * ncb
    grp = lambda r, c: pl.BlockSpec((gps, r, c), lambda sg, bp: (layer * nsg + sg, 0, 0))
    tok = pl.BlockSpec((nb * seq, LANES), lambda sg, bp: (bp, sg))
    return pl.pallas_call(
        functools.partial(_s5_kernel, chunk=chunk, ncb=ncb, nlev=nlev),
        out_shape=jax.ShapeDtypeStruct((bsz * seq, GROUP_WIDTH), F32),
        grid=(nsg, bsz // nb),
        in_specs=[tok, grp(width, width), grp(p2, width), grp(width, p2), grp(p2, p2), grp(p2, p2)],
        out_specs=tok,
        scratch_shapes=[pltpu.VMEM((gps, width, nchunks), BF16), pltpu.VMEM((gps, width, nchunks), F32)],
        compiler_params=_cparams(("parallel", "parallel"), 48),
        name="s5_mixer",
    )(u, a_op, mint, moutt, pa, pb)


def _ret_kernel(q_ref, k_ref, v_ref, g_ref, dec_ref, xi_ref, zeta_ref, gc_ref, bd_ref, o_ref, st_ref):
    @pl.when(pl.program_id(1) == 0)
    def _():
        st_ref[...] = jnp.zeros_like(st_ref)

    q = q_ref[...]
    k = k_ref[...]
    v = v_ref[...]
    lane_head = lax.broadcasted_iota(jnp.int32, (1, RET_HEADS * RET_QK), 1) // RET_QK
    st = st_ref[...]
    o_cross = jnp.dot(q, st.astype(BF16), preferred_element_type=F32) * xi_ref[...]
    outs = []
    for hd in range(RET_HEADS):
        km = jnp.where(lane_head == hd, k, jnp.zeros_like(k))
        s = lax.dot_general(q, km, (((1,), (1,)), ((), ())), preferred_element_type=F32)
        p = (s * dec_ref[hd]).astype(BF16)
        vh = v[:, hd * RET_V:(hd + 1) * RET_V]
        o = jnp.dot(p, vh, preferred_element_type=F32) + o_cross[:, hd * RET_V:(hd + 1) * RET_V]
        outs.append(o * lax.rsqrt(jnp.mean(o * o, axis=-1, keepdims=True) + EPS))
    on = jnp.concatenate(outs, axis=1)
    gate = g_ref[...].astype(F32)
    o_ref[...] = (on * (gate * jax.nn.sigmoid(gate))).astype(BF16)

    kz = (k.astype(F32) * zeta_ref[...]).astype(BF16)
    kv = lax.dot_general(kz, v, (((0,), (0,)), ((), ())), preferred_element_type=F32)
    st_ref[...] = st * gc_ref[...] + kv * bd_ref[...]


def _retention(rq, rk, rv, rg, bsz, seq):
    cb = min(RET_BLOCK, seq)
    nblk = seq // cb
    hq = RET_HEADS * RET_QK
    log_gamma = jnp.log1p(-(2.0 ** (-5.0 - jnp.arange(RET_HEADS, dtype=F32))))
    idx = jnp.arange(cb, dtype=F32)
    rel = idx[:, None] - idx[None, :]
    decay = jnp.where(rel >= 0, jnp.exp(log_gamma[:, None, None] * jnp.maximum(rel, 0.0)), 0.0)
    xi = jnp.repeat(jnp.exp(log_gamma[None, :] * (idx[:, None] + 1.0)), RET_V, axis=1)
    zeta = jnp.repeat(jnp.exp(log_gamma[None, :] * (cb - 1.0 - idx[:, None])), RET_QK, axis=1)
    gchunk = jnp.repeat(jnp.exp(log_gamma * cb), RET_V)[None, :]
    blockdiag = (jnp.arange(hq)[:, None] // RET_QK == jnp.arange(GROUP_WIDTH)[None, :] // RET_V).astype(F32)
    tok = lambda w: pl.BlockSpec((cb, w), lambda b, c: (b * nblk + c, 0))
    const = lambda shape: pl.BlockSpec(shape, lambda b, c: (0,) * len(shape))
    return pl.pallas_call(
        _ret_kernel,
        out_shape=jax.ShapeDtypeStruct((bsz * seq, GROUP_WIDTH), BF16),
        grid=(bsz, nblk),
        in_specs=[tok(hq), tok(hq), tok(GROUP_WIDTH), tok(GROUP_WIDTH),
                  const((RET_HEADS, cb, cb)), const((cb, GROUP_WIDTH)), const((cb, hq)),
                  const((1, GROUP_WIDTH)), const((hq, GROUP_WIDTH))],
        out_specs=tok(GROUP_WIDTH),
        scratch_shapes=[pltpu.VMEM((hq, GROUP_WIDTH), F32)],
        compiler_params=_cparams(("parallel", "arbitrary"), 32),
        name="retention",
    )(rq, rk, rv, rg, decay, xi, zeta, gchunk, blockdiag)


def _swa_kernel(sink_ref, q_ref, kc_ref, kp_ref, vc_ref, vp_ref, o_ref, *, nsub):
    blk = pl.program_id(1)
    kall = jnp.concatenate([kp_ref[...], kc_ref[...]], axis=0)
    vall = jnp.concatenate([vp_ref[...], vc_ref[...]], axis=0)
    half = lax.broadcasted_iota(jnp.int32, (1, 2 * SWA_HD), 1) // SWA_HD
    r = lax.broadcasted_iota(jnp.int32, (WINDOW, 2 * WINDOW), 0)
    jc = lax.broadcasted_iota(jnp.int32, (WINDOW, 2 * WINDOW), 1)
    dist = r + WINDOW - jc
    band = jnp.logical_and(dist >= 0, dist < WINDOW)
    first_key = jnp.where(blk > 0, 0, WINDOW)
    band_first = jnp.logical_and(band, jc >= first_key)
    for sb in range(nsub):
        valid = band_first if sb == 0 else band
        r0 = sb * WINDOW
        for j in range(SWA_KV_HEADS):
            kk = kall[r0:r0 + 2 * WINDOW, 128 * j:128 * (j + 1)]
            vv = vall[r0:r0 + 2 * WINDOW, 128 * j:128 * (j + 1)]
            zk = jnp.zeros_like(kk)
            kcat = jnp.concatenate([jnp.where(half == 0, kk, zk), jnp.where(half == 1, kk, zk)], axis=0)
            vcat = jnp.concatenate([jnp.where(half == 0, vv, zk), jnp.where(half == 1, vv, zk)], axis=0)
            q2 = jnp.concatenate([q_ref[r0:r0 + WINDOW, 256 * j:256 * j + 128],
                                  q_ref[r0:r0 + WINDOW, 256 * j + 128:256 * j + 256]], axis=0)
            s_all = lax.dot_general(q2, kcat, (((1,), (1,)), ((), ())), preferred_element_type=F32)
            p_rows, inv_rows = [], []
            for a in range(2):
                p_cols, invs = [], []
                for e in range(2):
                    sink = sink_ref[4 * j + 2 * a + e]
                    s = s_all[WINDOW * a:WINDOW * (a + 1), 2 * WINDOW * e:2 * WINDOW * (e + 1)]
                    s = jnp.where(valid, s, NEG)
                    m = jnp.maximum(jnp.max(s, axis=-1, keepdims=True), sink)
                    p = jnp.exp(s - m)
                    denom = jnp.sum(p, axis=-1, keepdims=True) + jnp.exp(sink - m)
                    p_cols.append(p.astype(BF16))
                    invs.append(1.0 / denom)
                p_rows.append(jnp.concatenate(p_cols, axis=1))
                inv_rows.append(jnp.where(half == 0, invs[0], invs[1]))
            p_all = jnp.concatenate(p_rows, axis=0)
            o = jnp.dot(p_all, vcat, preferred_element_type=F32)
            for a in range(2):
                o_ref[r0:r0 + WINDOW, 256 * j + 128 * a:256 * j + 128 * (a + 1)] = (
                    o[WINDOW * a:WINDOW * (a + 1)] * inv_rows[a]).astype(BF16)


def _swa(wq, wk2, wv2, sinks, bsz, seq, layer):
    qb = min(SWA_BLOCK, seq)
    nblk = seq // qb
    per = qb // WINDOW
    cur = lambda w: pl.BlockSpec((qb, w), lambda b, i: (b * nblk + i, 0))
    prev = lambda w: pl.BlockSpec((WINDOW, w), lambda b, i: (jnp.maximum((b * nblk + i) * per - 1, 0), 0))
    return pl.pallas_call(
        functools.partial(_swa_kernel, nsub=per),
        out_shape=jax.ShapeDtypeStruct((bsz * seq, GROUP_WIDTH), BF16),
        grid=(bsz, nblk),
        in_specs=[pl.BlockSpec(memory_space=pltpu.SMEM),
                  cur(GROUP_WIDTH), cur(256), prev(256), cur(256), prev(256)],
        out_specs=cur(GROUP_WIDTH),
        compiler_params=_cparams(("parallel", "parallel"), 32),
        name="swa",
    )(sinks[layer], wq, wk2, wk2, wv2, wv2)


def _mla_up_kernel(cq_ref, ckv_ref, kr_ref, qn_ref, kvn_ref, wq_ref, wkv_ref, cos_ref, sin_ref,
                   q_ref, k_ref, vt_ref):
    cq = cq_ref[...].astype(F32)
    nq = (cq * lax.rsqrt(jnp.mean(cq * cq, axis=-1, keepdims=True) + EPS) * qn_ref[...]).astype(BF16)
    qq = jnp.dot(nq, wq_ref[...], preferred_element_type=F32)
    ckv = ckv_ref[...].astype(F32)
    nkv = (ckv * lax.rsqrt(jnp.mean(ckv * ckv, axis=-1, keepdims=True) + EPS) * kvn_ref[...]).astype(BF16)
    kv = jnp.dot(nkv, wkv_ref[...], preferred_element_type=F32)
    cos, sin = cos_ref[...], sin_ref[...]
    kr = kr_ref[...].astype(F32)
    hw = MLA_QK_PAD
    for hd in range(MLA_HEADS):
        q_ref[:, hw * hd:hw * (hd + 1)] = (
            qq[:, hw * hd:hw * (hd + 1)] * cos
            + qq[:, hw * (MLA_HEADS + hd):hw * (MLA_HEADS + hd + 1)] * sin).astype(BF16)
        k_ref[:, hw * hd:hw * (hd + 1)] = (kv[:, hw * hd:hw * (hd + 1)] + kr).astype(BF16)
    vt_ref[...] = kv[:, hw * MLA_HEADS:].T.astype(BF16)


def _mla_up(cq, ckv, kr_pad, q_norm, kv_norm, wq_ext, wkv_ext, cosq, sinq, seq, layer):
    tokens = cq.shape[0]
    tm = min(MLA_TK, seq)
    per_seq = seq // tm
    hw = MLA_QK_PAD
    row = lambda w: pl.BlockSpec((tm, w), lambda i: (i, 0))
    tab = lambda: pl.BlockSpec((tm, hw), lambda i: (i % per_seq, 0))
    lay = lambda a: pl.BlockSpec((None,) + a.shape[1:], lambda i: (layer, 0, 0))
    return pl.pallas_call(
        _mla_up_kernel,
        out_shape=(jax.ShapeDtypeStruct((tokens, MLA_HEADS * hw), BF16),
                   jax.ShapeDtypeStruct((tokens, MLA_HEADS * hw), BF16),
                   jax.ShapeDtypeStruct((tokens // tm, MLA_HEADS * MLA_V, tm), BF16)),
        grid=(tokens // tm,),
        in_specs=[row(MLA_Q_RANK), row(MLA_KV_RANK), row(hw), lay(q_norm), lay(kv_norm),
                  lay(wq_ext), lay(wkv_ext), tab(), tab()],
        out_specs=(row(MLA_HEADS * hw), row(MLA_HEADS * hw),
                   pl.BlockSpec((None, MLA_HEADS * MLA_V, tm), lambda i: (i, 0, 0))),
        compiler_params=_cparams(("parallel",), 40),
        name="mla_up",
    )(cq, ckv, kr_pad, q_norm, kv_norm, wq_ext, wkv_ext, cosq, sinq)


def _mla_attn_kernel(q_ref, k_ref, vt_ref, o_ref, m_ref, l_ref, acc_ref, *, tq, tk, heads):
    qi = pl.program_id(2)
    hw = MLA_QK_PAD
    m_ref[...] = jnp.full_like(m_ref, NEG)
    l_ref[...] = jnp.zeros_like(l_ref)
    acc_ref[...] = jnp.zeros_like(acc_ref)

    def step(kb, masked):
        row0 = pl.multiple_of(kb * tk, tk)
        sts, ps, alphas = [], [], []
        for hd in range(heads):
            q = q_ref[:, hw * hd:hw * (hd + 1)]
            ks = k_ref[pl.ds(row0, tk), hw * hd:hw * (hd + 1)]
            st = lax.dot_general(ks, q, (((1,), (1,)), ((), ())), preferred_element_type=F32)
            if masked:
                key = lax.broadcasted_iota(jnp.int32, (tk, tq), 0)
                qry = lax.broadcasted_iota(jnp.int32, (tk, tq), 1)
                st = jnp.where(key <= qry, st, NEG)
            sts.append(st)
        for hd in range(heads):
            m_old = m_ref[hd]
            m_new = jnp.maximum(m_old, jnp.max(sts[hd], axis=0, keepdims=True))
            alpha = jnp.exp2(m_old - m_new)
            p = jnp.exp2(sts[hd] - m_new)
            l_ref[hd] = alpha * l_ref[hd] + jnp.sum(p, axis=0, keepdims=True)
            m_ref[hd] = m_new
            ps.append(p.astype(BF16))
            alphas.append(alpha)
        for hd in range(heads):
            vt = vt_ref[kb, MLA_V * hd:MLA_V * (hd + 1), :]
            acc_ref[hd] = alphas[hd] * acc_ref[hd] + jnp.dot(vt, ps[hd], preferred_element_type=F32)

    def body(kb, carry):
        step(kb, False)
        return carry

    lax.fori_loop(0, qi, body, 0)
    step(qi, True)
    for hd in range(heads):
        o_ref[:, MLA_V * hd:MLA_V * (hd + 1)] = (acc_ref[hd] / l_ref[hd]).T.astype(BF16)


def _mla_attention(q, k, vt, bsz, seq):
    tq = tk = min(MLA_TQ, seq)
    nq = seq // tq
    hp = MLA_HEADS_PER_STEP
    hw = MLA_QK_PAD
    return pl.pallas_call(
        functools.partial(_mla_attn_kernel, tq=tq, tk=tk, heads=hp),
        out_shape=jax.ShapeDtypeStruct((bsz * seq, MLA_HEADS * MLA_V), BF16),
        grid=(bsz, MLA_HEADS // hp, nq),
        in_specs=[pl.BlockSpec((tq, hp * hw), lambda b, h, i: (b * nq + i, h)),
                  pl.BlockSpec((seq, hp * hw), lambda b, h, i: (b, h)),
                  pl.BlockSpec((nq, hp * MLA_V, tk), lambda b, h, i: (b, h, 0))],
        out_specs=pl.BlockSpec((tq, hp * MLA_V), lambda b, h, i: (b * nq + i, h)),
        scratch_shapes=[pltpu.VMEM((hp, 1, tq), F32), pltpu.VMEM((hp, 1, tq), F32),
                        pltpu.VMEM((hp, MLA_V, tq), F32)],
        compiler_params=_cparams(("parallel", "parallel", "arbitrary"), 48),
        name="mla_attention",
    )(q, k, vt)


def _outproj_kernel(ys5_ref, yret_ref, yswa_ref, ymla_ref, h_ref, gt_ref, gw_ref, gb_ref, w_ref, o_ref,
                    *, per_seq):
    b = pl.program_id(0) // per_seq
    y = ys5_ref[...]
    z = jax.nn.gelu(y, approximate=True)
    gl = jnp.dot(z.astype(BF16), gw_ref[...], preferred_element_type=F32) + gb_ref[...]
    s5 = (z * jax.nn.sigmoid(gl)).astype(BF16)
    gw = GROUP_WIDTH
    mixed = jnp.dot(s5, w_ref[0:gw, :], preferred_element_type=F32)
    mixed += jnp.dot(yret_ref[...], w_ref[gw:2 * gw, :], preferred_element_type=F32)
    mixed += jnp.dot(yswa_ref[...], w_ref[2 * gw:3 * gw, :], preferred_element_type=F32)
    mixed += jnp.dot(ymla_ref[...], w_ref[3 * gw:4 * gw, :], preferred_element_type=F32)
    o_ref[...] = h_ref[...] + gt_ref[pl.ds(b, 1), :] * mixed


def _out_projection(ys5, yret, yswa, ymla, h2, mod, glu_w, glu_b, w_out, seq, layer):
    tokens, d = h2.shape
    tm = min(512, seq)
    per_seq = seq // tm
    gw = GROUP_WIDTH
    row = lambda w: pl.BlockSpec((tm, w), lambda i: (i, 0))
    return pl.pallas_call(
        functools.partial(_outproj_kernel, per_seq=per_seq),
        out_shape=jax.ShapeDtypeStruct((tokens, d), F32),
        grid=(tokens // tm,),
        in_specs=[row(gw), row(gw), row(gw), row(gw), row(d),
                  _mod_spec(layer, MOD_GT1, 1),
                  pl.BlockSpec((None, gw, gw), lambda i: (layer, 0, 0)),
                  _layer_vec(gw, layer, 1),
                  _resident((None, 4 * gw, d), lambda i: (layer, 0, 0))],
        out_specs=row(d),
        compiler_params=_cparams(("parallel",), 48),
        name="out_projection",
    )(ys5, yret, yswa, ymla, h2, mod, glu_w, glu_b, w_out)


def _mlp_kernel(h_ref, sc_ref, sh_ref, gt_ref, g_ref, fg_ref, w1_ref, w2_ref, o_ref, a_ref, *, per_seq, final):
    j = pl.program_id(1)
    b = pl.program_id(0) // per_seq

    @pl.when(j == 0)
    def _():
        x = h_ref[...]
        ms = jnp.mean(x * x, axis=-1, keepdims=True)
        y = x * lax.rsqrt(ms + EPS) * g_ref[...]
        a_ref[...] = (y * (1.0 + sc_ref[pl.ds(b, 1), :]) + sh_ref[pl.ds(b, 1), :]).astype(BF16)
        o_ref[...] = jnp.zeros_like(o_ref)

    hid = jnp.dot(a_ref[...], w1_ref[...], preferred_element_type=F32)
    hid = jnp.square(jnp.maximum(hid, 0.0)).astype(BF16)
    o_ref[...] += jnp.dot(hid, w2_ref[...], preferred_element_type=F32)

    @pl.when(j == pl.num_programs(1) - 1)
    def _():
        out = h_ref[...] + gt_ref[pl.ds(b, 1), :] * o_ref[...]
        if final:
            ms = jnp.mean(out * out, axis=-1, keepdims=True)
            out = out * lax.rsqrt(ms + EPS) * fg_ref[...]
        o_ref[...] = out


def _mlp(h2, mod, gains, final_gain, w1, w2, seq, layer, final):
    tokens, d = h2.shape
    dff = w1.shape[-1]
    tm = min(512, seq)
    tf = 1024
    per_seq = seq // tm
    return pl.pallas_call(
        functools.partial(_mlp_kernel, per_seq=per_seq, final=final),
        out_shape=jax.ShapeDtypeStruct((tokens, d), F32),
        grid=(tokens // tm, dff // tf),
        in_specs=[pl.BlockSpec((tm, d), lambda i, j: (i, 0)),
                  _mod_spec(layer, MOD_SC2, 2), _mod_spec(layer, MOD_SH2, 2), _mod_spec(layer, MOD_GT2, 2),
                  _layer_vec(d, layer, 2),
                  pl.BlockSpec((1, d), lambda i, j: (0, 0)),
                  pl.BlockSpec((None, d, tf), lambda i, j: (layer, 0, j)),
                  pl.BlockSpec((None, tf, d), lambda i, j: (layer, j, 0))],
        out_specs=pl.BlockSpec((tm, d), lambda i, j: (i, 0)),
        scratch_shapes=[pltpu.VMEM((tm, d), BF16)],
        compiler_params=_cparams(("parallel", "arbitrary"), 48),
        name="mlp",
    )(h2, mod, mod, mod, gains, final_gain, w1, w2)


def _swap_halves(w, heads, hd):
    lead = w.shape[:-1]
    w4 = w.reshape(lead + (heads, 2, hd // 2))
    return jnp.concatenate([w4[..., 1:, :], w4[..., :1, :]], axis=-2).reshape(lead + (heads * hd,))


def _mla_weight_ext(w_uq, w_ukv):
    hq = MLA_NOPE + MLA_ROPE
    hk = MLA_NOPE + MLA_V
    scale = hq ** -0.5 * math.log2(math.e)
    zq = lambda n: jnp.zeros(w_uq.shape[:-1] + (n,), w_uq.dtype)
    zk128 = jnp.zeros(w_ukv.shape[:-1] + (128,), w_ukv.dtype)
    plain, swapped, kcols, vcols = [], [], [], []
    for h in range(MLA_HEADS):
        plain += [w_uq[..., h * hq:(h + 1) * hq], zq(64)]
        swapped += [zq(128), _swap_halves(w_uq[..., h * hq + MLA_NOPE:(h + 1) * hq], 1, MLA_ROPE), zq(64)]
        kcols += [w_ukv[..., h * hk:h * hk + MLA_NOPE], zk128]
        vcols += [w_ukv[..., h * hk + MLA_NOPE:(h + 1) * hk]]
    wq_ext = (jnp.concatenate(plain + swapped, axis=-1) * scale).astype(BF16)
    wkv_ext = jnp.concatenate(kcols + vcols, axis=-1).astype(BF16)
    return wq_ext, wkv_ext


def _rotary_tables(seq):
    d = RET_QK
    inv = ROPE_BASE ** (-jnp.arange(0, d, 2, dtype=F32) / d)
    ang = jnp.arange(seq, dtype=F32)[:, None] * inv[None, :]
    cos, sin = jnp.cos(ang), jnp.sin(ang)
    cos1 = jnp.concatenate([cos, cos], axis=1)
    sin1 = jnp.concatenate([-sin, sin], axis=1)
    cos4, sin4 = jnp.tile(cos1, (1, 4)), jnp.tile(sin1, (1, 4))
    zeros64 = jnp.zeros((seq, 64), F32)
    cosq = jnp.concatenate([jnp.ones((seq, MLA_NOPE), F32), cos1, zeros64], axis=1)
    sinq = jnp.concatenate([jnp.zeros((seq, MLA_NOPE), F32), sin1, zeros64], axis=1)
    return cos4, sin4, cosq, sinq


def kernel(x, c, norm1_g, norm2_g, ada_w, ada_b, w_in, s5_lambda_re, s5_lambda_im, s5_log_dt, s5_b_re, s5_b_im, s5_c_re, s5_c_im, s5_d, s5_glu_w, s5_glu_b, swa_sinks, mla_q_norm, mla_kv_norm, mla_w_uq, mla_w_ukv, w_out, mlp_w1, mlp_w2, final_norm_g):
    bsz, seq, d = x.shape
    depth = ada_w.shape[0]
    tokens = bsz * seq
    h = x.reshape(tokens, d)
    mod = _adaln_mod(c, ada_w, ada_b)
    cos4, sin4, cosq, sinq = _rotary_tables(seq)
    w_ext = w_in.astype(BF16)
    wq_ext, wkv_ext = _mla_weight_ext(mla_w_uq, mla_w_ukv)
    w_out_b, glu_w_b = w_out.astype(BF16), s5_glu_w.astype(BF16)
    w1_b, w2_b = mlp_w1.astype(BF16), mlp_w2.astype(BF16)
    norm1 = norm1_g.reshape(depth, 1, d)
    norm2 = norm2_g.reshape(depth, 1, d)
    glu_b = s5_glu_b.reshape(depth, 1, -1)
    q_norm = mla_q_norm.reshape(depth, 1, -1)
    kv_norm = mla_kv_norm.reshape(depth, 1, -1)
    ncb = seq // S5_CHUNK
    s5_ops = _s5_prepare(s5_lambda_re, s5_lambda_im, s5_log_dt, s5_b_re, s5_b_im, s5_c_re, s5_c_im, s5_d,
                         S5_CHUNK, max(1, (ncb - 1).bit_length()))
    for l in range(depth):
        (u, rq, rk, rv, rg, wq, wk2, wv2, cq, ckv, kr_pad) = _in_projection(
            h, mod, norm1, w_ext, cos4, sin4, seq, l)
        y_s5 = _s5_mixer_pre_glu(u, s5_ops, bsz, seq, l)
        y_ret = _retention(rq, rk, rv, rg, bsz, seq)
        y_swa = _swa(wq, wk2, wv2, swa_sinks, bsz, seq, l)
        mq, mk, mvt = _mla_up(cq, ckv, kr_pad, q_norm, kv_norm, wq_ext, wkv_ext, cosq, sinq, seq, l)
        y_mla = _mla_attention(mq, mk, mvt, bsz, seq)
        h = _out_projection(y_s5, y_ret, y_swa, y_mla, h, mod, glu_w_b, glu_b, w_out_b, seq, l)
        h = _mlp(h, mod, norm2, final_norm_g.reshape(1, d), w1_b, w2_b, seq, l, final=(l == depth - 1))
    return h.reshape(bsz, seq, d)
```

```python
import functools
import math

import numpy as np
import jax
import jax.numpy as jnp
from jax import lax
from jax.experimental import pallas as pl
from jax.experimental.pallas import tpu as pltpu

F32 = jnp.float32
BF16 = jnp.bfloat16

D_MODEL = 2048
GROUP_WIDTH = 512
S5_CH = 16
S5_GROUPS = GROUP_WIDTH // S5_CH
S5_STATE = 64
RET_HEADS = 4
RET_QK = 64
RET_V = 128
SWA_HD = 64
SWA_HEADS = 8
SWA_KV_HEADS = 2
WINDOW = 128
MLA_HEADS = 4
MLA_Q_RANK = 384
MLA_KV_RANK = 128
MLA_NOPE = 128
MLA_ROPE = 64
MLA_V = 128
MLA_QK_PAD = 256
ROPE_BASE = 10000.0
EPS = 1e-6
NEG = -1e30

MIB = 1024 * 1024
LANES = 128
MOD_ROWS = 8
NORM_ROWS = 16
S5_CHUNK = 32
S5_GROUPS_PER_STEP = LANES // S5_CH
S5_BATCH_PER_STEP = 2
RET_BLOCK = 512
SWA_BLOCK = 512
MLA_TQ = 512
MLA_TK = 512
MLA_HEADS_PER_STEP = 4
MOD_SH1, MOD_SC1, MOD_GT1, MOD_SH2, MOD_SC2, MOD_GT2 = range(6)

_O_U, _O_RQ, _O_RK, _O_RV, _O_RG, _O_WQ, _O_WK, _O_WV, _O_CQ, _O_CKV, _O_KR, _O_END = (
    0, 512, 768, 1024, 1536, 2048, 2560, 2688, 2816, 3200, 3328, 3392)


def _cparams(sem, vmem_mib):
    return pltpu.CompilerParams(dimension_semantics=sem, vmem_limit_bytes=int(vmem_mib * MIB))


def _resident(block_shape, index_map):
    return pl.BlockSpec(block_shape, index_map, pipeline_mode=pl.Buffered(1))


def _mod_spec(layer, which, ngrid):
    if ngrid == 1:
        return pl.BlockSpec((None, None, MOD_ROWS, D_MODEL), lambda i: (layer, which, 0, 0))
    return pl.BlockSpec((None, None, MOD_ROWS, D_MODEL), lambda i, j: (layer, which, 0, 0))


def _norm_modulate(x_ref, gain, shift, out_ref):
    for r in range(0, x_ref.shape[0], NORM_ROWS):
        x = x_ref[r:r + NORM_ROWS, :]
        ms = jnp.mean(x * x, axis=-1, keepdims=True)
        out_ref[r:r + NORM_ROWS, :] = ((x * lax.rsqrt(ms + EPS)) * gain + shift).astype(out_ref.dtype)


def _layer_vec(width, layer, ngrid):
    if ngrid == 1:
        return pl.BlockSpec((None, 1, width), lambda i: (layer, 0, 0))
    return pl.BlockSpec((None, 1, width), lambda i, j: (layer, 0, 0))


def _mod_kernel(c_ref, w_ref, b_ref, o_ref):
    c = c_ref[...]
    ca = (c * jax.nn.sigmoid(c)).astype(BF16)
    o_ref[...] = jnp.dot(ca, w_ref[...].astype(BF16), preferred_element_type=F32) + b_ref[...]


def _adaln_mod(c, ada_w, ada_b):
    depth, d, n = ada_w.shape
    bsz = c.shape[0]
    cp = jnp.zeros((MOD_ROWS, d), F32).at[:bsz].set(c)
    tn = 1024
    per_vec = d // tn
    return pl.pallas_call(
        _mod_kernel,
        out_shape=jax.ShapeDtypeStruct((depth, n // d, MOD_ROWS, d), F32),
        grid=(depth, n // tn),
        in_specs=[
            pl.BlockSpec((MOD_ROWS, d), lambda l, j: (0, 0)),
            pl.BlockSpec((None, d, tn), lambda l, j: (l, 0, j)),
            pl.BlockSpec((None, 1, tn), lambda l, j: (l, 0, j)),
        ],
        out_specs=pl.BlockSpec((None, None, MOD_ROWS, tn), lambda l, j: (l, j // per_vec, 0, j % per_vec)),
        compiler_params=_cparams(("parallel", "parallel"), 40),
        name="adaln_mod",
    )(cp, ada_w, ada_b.reshape(depth, 1, n))


def _inproj_kernel(h_ref, sc_ref, sh_ref, g_ref, w_ref, cos_ref, sin_ref, swap_ref, krsel_ref, dup_ref,
                   u_ref, rq_ref, rk_ref, rv_ref, rg_ref, wq_ref, wk_ref, wv_ref,
                   cq_ref, ckv_ref, kr_ref, a_ref, *, per_seq, parts):
    b = pl.program_id(0) // per_seq
    rows = h_ref.shape[0] // parts
    _norm_modulate(h_ref, g_ref[...] * (1.0 + sc_ref[pl.ds(b, 1), :]), sh_ref[pl.ds(b, 1), :], a_ref)
    projs = [jnp.dot(a_ref[rows * t:rows * (t + 1), :], w_ref[...], preferred_element_type=F32)
             for t in range(parts)]
    swap = swap_ref[...]
    dup = dup_ref[...]
    for t in range(parts):
        p = projs[t]
        rs = slice(rows * t, rows * (t + 1))
        cos = cos_ref[rs, :]
        sin = sin_ref[rs, :]

        def rope(v):
            return v * cos + jnp.dot(v.astype(BF16), swap, preferred_element_type=F32) * sin

        u_ref[rs, :] = p[:, _O_U:_O_RQ]
        rq_ref[rs, :] = rope(p[:, _O_RQ:_O_RK]).astype(BF16)
        rk_ref[rs, :] = rope(p[:, _O_RK:_O_RV] * (RET_QK ** -0.5)).astype(BF16)
        rv_ref[rs, :] = p[:, _O_RV:_O_RG].astype(BF16)
        rg_ref[rs, :] = p[:, _O_RG:_O_WQ].astype(BF16)
        wq_ref[rs, :] = (p[:, _O_WQ:_O_WK] * (SWA_HD ** -0.5)).astype(BF16)
        wk_ref[rs, :] = jnp.dot(p[:, _O_WK:_O_WV].astype(BF16), dup, preferred_element_type=F32).astype(BF16)
        wv_ref[rs, :] = jnp.dot(p[:, _O_WV:_O_CQ].astype(BF16), dup, preferred_element_type=F32).astype(BF16)
        cq_ref[rs, :] = p[:, _O_CQ:_O_CKV].astype(BF16)
        ckv_ref[rs, :] = p[:, _O_CKV:_O_KR].astype(BF16)
        kr2 = jnp.dot(p[:, _O_KR:_O_END].astype(BF16), krsel_ref[...], preferred_element_type=F32)
        kr = kr2[:, :128] * cos[:, :128] + kr2[:, 128:] * sin[:, :128]
        kr_ref[rs, 0:128] = jnp.zeros_like(kr).astype(BF16)
        kr_ref[rs, 128:256] = kr.astype(BF16)


def _lane_maps():
    half = RET_QK // 2
    j = np.arange(RET_HEADS * RET_QK)
    swap = np.zeros((j.size, j.size), np.float32)
    swap[(j // RET_QK) * RET_QK + (j % RET_QK + half) % RET_QK, j] = 1.0
    i = np.arange(MLA_ROPE)
    krsel = np.zeros((MLA_ROPE, 256), np.float32)
    krsel[i, i] = 1.0
    krsel[(i + half) % MLA_ROPE, 128 + i] = 1.0
    c = np.arange(256)
    dup = np.zeros((128, 256), np.float32)
    dup[(c // 128) * SWA_HD + c % SWA_HD, c] = 1.0
    return jnp.asarray(swap, BF16), jnp.asarray(krsel, BF16), jnp.asarray(dup, BF16)


def _in_projection(h2, mod, gains, w_in_b, cos4, sin4, seq, layer):
    tokens, d = h2.shape
    n_in = w_in_b.shape[-1]
    tm = 512
    per_seq = seq // tm
    widths = (512, 256, 256, 512, 512, 512, 256, 256, 384, 128, 256)
    dtypes = (F32,) + (BF16,) * 10
    row_blk = lambda w: pl.BlockSpec((tm, w), lambda i: (i, 0))
    const = lambda a: pl.BlockSpec(a.shape, lambda i: (0, 0))
    swap, krsel, dup = _lane_maps()
    return pl.pallas_call(
        functools.partial(_inproj_kernel, per_seq=per_seq, parts=1),
        out_shape=tuple(jax.ShapeDtypeStruct((tokens, w), t) for w, t in zip(widths, dtypes)),
        grid=(tokens // tm,),
        in_specs=[
            row_blk(d),
            _mod_spec(layer, MOD_SC1, 1),
            _mod_spec(layer, MOD_SH1, 1),
            _layer_vec(d, layer, 1),
            _resident((None, d, n_in), lambda i: (layer, 0, 0)),
            pl.BlockSpec((tm, 256), lambda i: (i % per_seq, 0)),
            pl.BlockSpec((tm, 256), lambda i: (i % per_seq, 0)),
            const(swap), const(krsel), const(dup),
        ],
        out_specs=tuple(row_blk(w) for w in widths),
        scratch_shapes=[pltpu.VMEM((tm, d), BF16)],
        compiler_params=_cparams(("parallel",), 56),
        name="in_projection",
    )(h2, mod, mod, gains, w_in_b, cos4, sin4, swap, krsel, dup)


def _s5_prep_kernel(lr_ref, li_ref, ldt_ref, btr_ref, bti_ref, cr_ref, ci_ref, dlag0_ref,
                    a_ref, mint_ref, moutt_ref, pa_ref, pb_ref, *, chunk, nlev):
    width = chunk * S5_CH
    lr = lr_ref[...]
    li = li_ref[...]
    dt = jnp.exp(ldt_ref[...])
    btr, bti = btr_ref[...], bti_ref[...]
    cre, cim = cr_ref[...], ci_ref[...]

    def cpow(k):
        mag = jnp.exp(lr * dt * k)
        ang = li * dt * k
        return mag * jnp.cos(ang), mag * jnp.sin(ang)

    one = jnp.ones((1, 1), F32)
    ar, ai = cpow(one)
    den = lr * lr + li * li
    kr = ((ar - 1.0) * lr + ai * li) / den
    ki = (ai * lr - (ar - 1.0) * li) / den
    bbr = kr * btr - ki * bti
    bbi = kr * bti + ki * btr

    kidx = lax.broadcasted_iota(jnp.int32, (chunk, 1), 0).astype(F32)

    def outer(pr, pi, mr, mi):
        re = pr[:, None, :] * mr[None, :, :] - pi[:, None, :] * mi[None, :, :]
        im = pr[:, None, :] * mi[None, :, :] + pi[:, None, :] * mr[None, :, :]
        return re.reshape(width, S5_STATE), im.reshape(width, S5_STATE)

    er, ei = cpow((chunk - 1.0) - kidx)
    m_r, m_i = outer(er, ei, bbr, bbi)
    m_in = jnp.concatenate([m_r, m_i], axis=1)
    mint_ref[...] = m_in.T.astype(BF16)

    fr, fi = cpow(kidx + 1.0)
    w_r, w_i = outer(fr, fi, cre, cim)
    moutt_ref[...] = jnp.concatenate([w_r, -w_i], axis=1).astype(BF16)

    cc = jnp.concatenate([cre, -cim], axis=1)
    ks = lax.dot_general(cc, m_in, (((1,), (1,)), ((), ())), precision=lax.Precision.HIGHEST,
                         preferred_element_type=F32)
    ks = ks + dlag0_ref[...]
    lane = lax.broadcasted_iota(jnp.int32, (S5_CH, width), 1)
    for t in range(chunk):
        if t == chunk - 1:
            rows = ks
        else:
            shift = (width - S5_CH * (chunk - 1 - t)) % width
            rows = jnp.where(lane < S5_CH * (t + 1), pltpu.roll(ks, shift, axis=1), 0.0)
        a_ref[S5_CH * t:S5_CH * (t + 1), :] = rows.astype(BF16)

    lev = lax.broadcasted_iota(jnp.int32, (8, 1), 0)
    pr, pi = cpow((chunk * jnp.left_shift(1, jnp.minimum(lev, nlev))).astype(F32))
    reps = (2 * S5_STATE) // 8
    pa_ref[...] = jnp.tile(jnp.concatenate([pr, pr], axis=1), (reps, 1)).T
    pb_ref[...] = jnp.tile(jnp.concatenate([-pi, pi], axis=1), (reps, 1)).T


def _s5_prepare(lam_re, lam_im, log_dt, b_re, b_im, c_re, c_im, d_skip, chunk, nlev):
    p = lam_re.shape[-1]
    h = S5_CH
    g = lam_re.shape[0] * lam_re.shape[1]
    width = chunk * h
    d_lag0 = jnp.concatenate([jnp.zeros((g, h, width - h), F32),
                              d_skip.reshape(g, h, 1) * jnp.eye(h, dtype=F32)], axis=-1)
    p2 = 2 * p
    vec = lambda: pl.BlockSpec((None, 1, p), lambda i: (i, 0, 0))
    mat = lambda: pl.BlockSpec((None, h, p), lambda i: (i, 0, 0))
    out = lambda r, c: pl.BlockSpec((None, r, c), lambda i: (i, 0, 0))
    return pl.pallas_call(
        functools.partial(_s5_prep_kernel, chunk=chunk, nlev=nlev),
        out_shape=(
            jax.ShapeDtypeStruct((g, width, width), BF16),
            jax.ShapeDtypeStruct((g, p2, width), BF16),
            jax.ShapeDtypeStruct((g, width, p2), BF16),
            jax.ShapeDtypeStruct((g, p2, p2), F32),
            jax.ShapeDtypeStruct((g, p2, p2), F32),
        ),
        grid=(g,),
        in_specs=[vec(), vec(), vec(), mat(), mat(), mat(), mat(),
                  pl.BlockSpec((None, h, width), lambda i: (i, 0, 0))],
        out_specs=(out(width, width), out(p2, width), out(width, p2), out(p2, p2), out(p2, p2)),
        compiler_params=_cparams(("parallel",), 32),
        name="s5_prepare",
    )(lam_re.reshape(g, 1, p), lam_im.reshape(g, 1, p),
      jnp.broadcast_to(log_dt.reshape(g, 1, 1), (g, 1, p)),
      jnp.swapaxes(b_re, -1, -2).reshape(g, h, p), jnp.swapaxes(b_im, -1, -2).reshape(g, h, p),
      c_re.reshape(g, h, p), c_im.reshape(g, h, p), d_lag0)


def _s5_kernel(u_ref, a_ref, mint_ref, moutt_ref, pa_ref, pb_ref, eye_c_ref, eye_r_ref, y_ref, ut_ref, yt_ref,
               *, chunk, ncb, nlev):
    gps = S5_GROUPS_PER_STEP
    nchunks = ut_ref.shape[-1]
    nt = (((1,), (1,)), ((), ()))
    eye_c, eye_r = eye_c_ref[...], eye_r_ref[...]
    for s in range(chunk):
        slab = u_ref[pl.ds(s, nchunks, stride=chunk), :].astype(BF16)
        slab_t = lax.dot_general(eye_c, slab, nt, preferred_element_type=F32)
        for g in range(gps):
            ut_ref[g, S5_CH * s:S5_CH * (s + 1), :] = slab_t[S5_CH * g:S5_CH * (g + 1), :].astype(BF16)

    cidx = lax.broadcasted_iota(jnp.int32, (1, nchunks), 1) % ncb
    xs = []
    for g in range(gps):
        ut = ut_ref[g]
        yt_ref[g] = jnp.dot(a_ref[g], ut, preferred_element_type=F32)
        xs.append(jnp.dot(mint_ref[g], ut, preferred_element_type=F32))
    for k in range(nlev):
        d = 1 << k
        for g in range(gps):
            sh = jnp.where(cidx >= d, pltpu.roll(xs[g], d, axis=1), 0.0)
            shs = jnp.concatenate([sh[S5_STATE:], sh[:S5_STATE]], axis=0)
            xs[g] = xs[g] + pa_ref[g, :, k:k + 1] * sh + pb_ref[g, :, k:k + 1] * shs
    for g in range(gps):
        xp = jnp.where(cidx >= 1, pltpu.roll(xs[g], 1, axis=1), 0.0)
        yt_ref[g] += jnp.dot(moutt_ref[g], xp.astype(BF16), preferred_element_type=F32)

    for s in range(chunk):
        z = jnp.concatenate([yt_ref[g, S5_CH * s:S5_CH * (s + 1), :] for g in range(gps)], axis=0)
        z_hi = z.astype(BF16)
        z_lo = (z - z_hi.astype(F32)).astype(BF16)
        y_ref[pl.ds(s, nchunks, stride=chunk), :] = (
            lax.dot_general(eye_r, z_hi, nt, preferred_element_type=F32)
            + lax.dot_general(eye_r, z_lo, nt, preferred_element_type=F32))


def _s5_mixer_pre_glu(u, ops, bsz, seq, layer):
    a_op, mint, moutt, pa, pb = ops
    chunk = S5_CHUNK
    ncb = seq // chunk
    nlev = max(1, (ncb - 1).bit_length())
    nb = min(S5_BATCH_PER_STEP, bsz)
    gps = S5_GROUPS_PER_STEP
    nsg = S5_GROUPS // gps
    width = chunk * S5_CH
    p2 = 2 * S5_STATE
    nchunks = nb * ncb
    grp = lambda r, c: pl.BlockSpec((gps, r, c), lambda sg, bp: (layer * nsg + sg, 0, 0))
    tok = pl.BlockSpec((nb * seq, LANES), lambda sg, bp: (bp, sg))
    return pl.pallas_call(
        functools.partial(_s5_kernel, chunk=chunk, ncb=ncb, nlev=nlev),
        out_shape=jax.ShapeDtypeStruct((bsz * seq, GROUP_WIDTH), F32),
        grid=(nsg, bsz // nb),
        in_specs=[tok, grp(width, width), grp(p2, width), grp(width, p2), grp(p2, p2), grp(p2, p2),
                  pl.BlockSpec((LANES, LANES), lambda sg, bp: (0, 0)),
                  pl.BlockSpec((nchunks, nchunks), lambda sg, bp: (0, 0))],
        out_specs=tok,
        scratch_shapes=[pltpu.VMEM((gps, width, nchunks), BF16), pltpu.VMEM((gps, width, nchunks), F32)],
        compiler_params=_cparams(("parallel", "parallel"), 48),
        name="s5_mixer",
    )(u, a_op, mint, moutt, pa, pb, jnp.eye(LANES, dtype=BF16), jnp.eye(nchunks, dtype=BF16))


def _ret_kernel(q_ref, k_ref, v_ref, g_ref, dec_ref, xi_ref, zeta_ref, gc_ref, bd_ref, o_ref, st_ref):
    @pl.when(pl.program_id(1) == 0)
    def _():
        st_ref[...] = jnp.zeros_like(st_ref)

    q = q_ref[...]
    k = k_ref[...]
    v = v_ref[...]
    lane_head = lax.broadcasted_iota(jnp.int32, (1, RET_HEADS * RET_QK), 1) // RET_QK
    st = st_ref[...]
    o_cross = jnp.dot(q, st.astype(BF16), preferred_element_type=F32) * xi_ref[...]
    outs = []
    for hd in range(RET_HEADS):
        km = jnp.where(lane_head == hd, k, jnp.zeros_like(k))
        s = lax.dot_general(q, km, (((1,), (1,)), ((), ())), preferred_element_type=F32)
        p = (s * dec_ref[hd]).astype(BF16)
        vh = v[:, hd * RET_V:(hd + 1) * RET_V]
        o = jnp.dot(p, vh, preferred_element_type=F32) + o_cross[:, hd * RET_V:(hd + 1) * RET_V]
        outs.append(o * lax.rsqrt(jnp.mean(o * o, axis=-1, keepdims=True) + EPS))
    on = jnp.concatenate(outs, axis=1)
    gate = g_ref[...].astype(F32)
    o_ref[...] = (on * (gate * jax.nn.sigmoid(gate))).astype(BF16)

    kz = (k.astype(F32) * zeta_ref[...]).astype(BF16)
    kv = lax.dot_general(kz, v, (((0,), (0,)), ((), ())), preferred_element_type=F32)
    st_ref[...] = st * gc_ref[...] + kv * bd_ref[...]


def _retention(rq, rk, rv, rg, bsz, seq):
    cb = min(RET_BLOCK, seq)
    nblk = seq // cb
    hq = RET_HEADS * RET_QK
    log_gamma = jnp.log1p(-(2.0 ** (-5.0 - jnp.arange(RET_HEADS, dtype=F32))))
    idx = jnp.arange(cb, dtype=F32)
    rel = idx[:, None] - idx[None, :]
    decay = jnp.where(rel >= 0, jnp.exp(log_gamma[:, None, None] * jnp.maximum(rel, 0.0)), 0.0)
    xi = jnp.repeat(jnp.exp(log_gamma[None, :] * (idx[:, None] + 1.0)), RET_V, axis=1)
    zeta = jnp.repeat(jnp.exp(log_gamma[None, :] * (cb - 1.0 - idx[:, None])), RET_QK, axis=1)
    gchunk = jnp.repeat(jnp.exp(log_gamma * cb), RET_V)[None, :]
    blockdiag = (jnp.arange(hq)[:, None] // RET_QK == jnp.arange(GROUP_WIDTH)[None, :] // RET_V).astype(F32)
    tok = lambda w: pl.BlockSpec((cb, w), lambda b, c: (b * nblk + c, 0))
    const = lambda shape: pl.BlockSpec(shape, lambda b, c: (0,) * len(shape))
    return pl.pallas_call(
        _ret_kernel,
        out_shape=jax.ShapeDtypeStruct((bsz * seq, GROUP_WIDTH), BF16),
        grid=(bsz, nblk),
        in_specs=[tok(hq), tok(hq), tok(GROUP_WIDTH), tok(GROUP_WIDTH),
                  const((RET_HEADS, cb, cb)), const((cb, GROUP_WIDTH)), const((cb, hq)),
                  const((1, GROUP_WIDTH)), const((hq, GROUP_WIDTH))],
        out_specs=tok(GROUP_WIDTH),
        scratch_shapes=[pltpu.VMEM((hq, GROUP_WIDTH), F32)],
        compiler_params=_cparams(("parallel", "arbitrary"), 32),
        name="retention",
    )(rq, rk, rv, rg, decay, xi, zeta, gchunk, blockdiag)


def _swa_kernel(sink_ref, q_ref, kc_ref, kp_ref, vc_ref, vp_ref, o_ref, *, nsub):
    blk = pl.program_id(1)
    kall = jnp.concatenate([kp_ref[...], kc_ref[...]], axis=0)
    vall = jnp.concatenate([vp_ref[...], vc_ref[...]], axis=0)
    half = lax.broadcasted_iota(jnp.int32, (1, 2 * SWA_HD), 1) // SWA_HD
    r = lax.broadcasted_iota(jnp.int32, (WINDOW, 2 * WINDOW), 0)
    jc = lax.broadcasted_iota(jnp.int32, (WINDOW, 2 * WINDOW), 1)
    dist = r + WINDOW - jc
    band = jnp.logical_and(dist >= 0, dist < WINDOW)
    first_key = jnp.where(blk > 0, 0, WINDOW)
    band_first = jnp.logical_and(band, jc >= first_key)
    for sb in range(nsub):
        valid = band_first if sb == 0 else band
        r0 = sb * WINDOW
        for j in range(SWA_KV_HEADS):
            kk = kall[r0:r0 + 2 * WINDOW, 128 * j:128 * (j + 1)]
            vv = vall[r0:r0 + 2 * WINDOW, 128 * j:128 * (j + 1)]
            zk = jnp.zeros_like(kk)
            kcat = jnp.concatenate([jnp.where(half == 0, kk, zk), jnp.where(half == 1, kk, zk)], axis=0)
            vcat = jnp.concatenate([jnp.where(half == 0, vv, zk), jnp.where(half == 1, vv, zk)], axis=0)
            q2 = jnp.concatenate([q_ref[r0:r0 + WINDOW, 256 * j:256 * j + 128],
                                  q_ref[r0:r0 + WINDOW, 256 * j + 128:256 * j + 256]], axis=0)
            s_all = lax.dot_general(q2, kcat, (((1,), (1,)), ((), ())), preferred_element_type=F32)
            p_rows, inv_rows = [], []
            for a in range(2):
                p_cols, invs = [], []
                for e in range(2):
                    sink = sink_ref[4 * j + 2 * a + e]
                    s = s_all[WINDOW * a:WINDOW * (a + 1), 2 * WINDOW * e:2 * WINDOW * (e + 1)]
                    s = jnp.where(valid, s, NEG)
                    m = jnp.maximum(jnp.max(s, axis=-1, keepdims=True), sink)
                    p = jnp.exp(s - m)
                    denom = jnp.sum(p, axis=-1, keepdims=True) + jnp.exp(sink - m)
                    p_cols.append(p.astype(BF16))
                    invs.append(1.0 / denom)
                p_rows.append(jnp.concatenate(p_cols, axis=1))
                inv_rows.append(jnp.where(half == 0, invs[0], invs[1]))
            p_all = jnp.concatenate(p_rows, axis=0)
            o = jnp.dot(p_all, vcat, preferred_element_type=F32)
            for a in range(2):
                o_ref[r0:r0 + WINDOW, 256 * j + 128 * a:256 * j + 128 * (a + 1)] = (
                    o[WINDOW * a:WINDOW * (a + 1)] * inv_rows[a]).astype(BF16)


def _swa(wq, wk2, wv2, sinks, bsz, seq, layer):
    qb = min(SWA_BLOCK, seq)
    nblk = seq // qb
    per = qb // WINDOW
    cur = lambda w: pl.BlockSpec((qb, w), lambda b, i: (b * nblk + i, 0))
    prev = lambda w: pl.BlockSpec((WINDOW, w), lambda b, i: (jnp.maximum((b * nblk + i) * per - 1, 0), 0))
    return pl.pallas_call(
        functools.partial(_swa_kernel, nsub=per),
        out_shape=jax.ShapeDtypeStruct((bsz * seq, GROUP_WIDTH), BF16),
        grid=(bsz, nblk),
        in_specs=[pl.BlockSpec(memory_space=pltpu.SMEM),
                  cur(GROUP_WIDTH), cur(256), prev(256), cur(256), prev(256)],
        out_specs=cur(GROUP_WIDTH),
        compiler_params=_cparams(("parallel", "parallel"), 32),
        name="swa",
    )(sinks[layer], wq, wk2, wk2, wv2, wv2)


def _mla_up_kernel(cq_ref, ckv_ref, kr_ref, qn_ref, kvn_ref, wq_ref, wkv_ref, cos_ref, sin_ref,
                   q_ref, k_ref, vt_ref):
    cq = cq_ref[...].astype(F32)
    nq = (cq * lax.rsqrt(jnp.mean(cq * cq, axis=-1, keepdims=True) + EPS) * qn_ref[...]).astype(BF16)
    qq = jnp.dot(nq, wq_ref[...], preferred_element_type=F32)
    ckv = ckv_ref[...].astype(F32)
    nkv = (ckv * lax.rsqrt(jnp.mean(ckv * ckv, axis=-1, keepdims=True) + EPS) * kvn_ref[...]).astype(BF16)
    kv = jnp.dot(nkv, wkv_ref[...], preferred_element_type=F32)
    cos, sin = cos_ref[...], sin_ref[...]
    kr = kr_ref[...].astype(F32)
    hw = MLA_QK_PAD
    for hd in range(MLA_HEADS):
        q_ref[:, hw * hd:hw * (hd + 1)] = (
            qq[:, hw * hd:hw * (hd + 1)] * cos
            + qq[:, hw * (MLA_HEADS + hd):hw * (MLA_HEADS + hd + 1)] * sin).astype(BF16)
        k_ref[:, hw * hd:hw * (hd + 1)] = (kv[:, hw * hd:hw * (hd + 1)] + kr).astype(BF16)
    vt_ref[...] = kv[:, hw * MLA_HEADS:].T.astype(BF16)


def _mla_up(cq, ckv, kr_pad, q_norm, kv_norm, wq_ext, wkv_ext, cosq, sinq, seq, layer):
    tokens = cq.shape[0]
    tm = min(MLA_TK, seq)
    per_seq = seq // tm
    hw = MLA_QK_PAD
    row = lambda w: pl.BlockSpec((tm, w), lambda i: (i, 0))
    tab = lambda: pl.BlockSpec((tm, hw), lambda i: (i % per_seq, 0))
    lay = lambda a: pl.BlockSpec((None,) + a.shape[1:], lambda i: (layer, 0, 0))
    return pl.pallas_call(
        _mla_up_kernel,
        out_shape=(jax.ShapeDtypeStruct((tokens, MLA_HEADS * hw), BF16),
                   jax.ShapeDtypeStruct((tokens, MLA_HEADS * hw), BF16),
                   jax.ShapeDtypeStruct((tokens // tm, MLA_HEADS * MLA_V, tm), BF16)),
        grid=(tokens // tm,),
        in_specs=[row(MLA_Q_RANK), row(MLA_KV_RANK), row(hw), lay(q_norm), lay(kv_norm),
                  lay(wq_ext), lay(wkv_ext), tab(), tab()],
        out_specs=(row(MLA_HEADS * hw), row(MLA_HEADS * hw),
                   pl.BlockSpec((None, MLA_HEADS * MLA_V, tm), lambda i: (i, 0, 0))),
        compiler_params=_cparams(("parallel",), 40),
        name="mla_up",
    )(cq, ckv, kr_pad, q_norm, kv_norm, wq_ext, wkv_ext, cosq, sinq)


def _mla_attn_kernel(q_ref, k_ref, vt_ref, o_ref, m_ref, l_ref, acc_ref, *, tq, tk, heads):
    qi = pl.program_id(2)
    hw = MLA_QK_PAD
    m_ref[...] = jnp.full_like(m_ref, NEG)
    l_ref[...] = jnp.zeros_like(l_ref)
    acc_ref[...] = jnp.zeros_like(acc_ref)

    def step(kb, masked):
        row0 = pl.multiple_of(kb * tk, tk)
        sts, ps, alphas = [], [], []
        for hd in range(heads):
            q = q_ref[:, hw * hd:hw * (hd + 1)]
            ks = k_ref[pl.ds(row0, tk), hw * hd:hw * (hd + 1)]
            st = lax.dot_general(ks, q, (((1,), (1,)), ((), ())), preferred_element_type=F32)
            if masked:
                key = lax.broadcasted_iota(jnp.int32, (tk, tq), 0)
                qry = lax.broadcasted_iota(jnp.int32, (tk, tq), 1)
                st = jnp.where(key <= qry, st, NEG)
            sts.append(st)
        for hd in range(heads):
            m_old = m_ref[hd]
            m_new = jnp.maximum(m_old, jnp.max(sts[hd], axis=0, keepdims=True))
            alpha = jnp.exp2(m_old - m_new)
            p = jnp.exp2(sts[hd] - m_new)
            l_ref[hd] = alpha * l_ref[hd] + jnp.sum(p, axis=0, keepdims=True)
            m_ref[hd] = m_new
            ps.append(p.astype(BF16))
            alphas.append(alpha)
        for hd in range(heads):
            vt = vt_ref[kb, MLA_V * hd:MLA_V * (hd + 1), :]
            acc_ref[hd] = alphas[hd] * acc_ref[hd] + jnp.dot(vt, ps[hd], preferred_element_type=F32)

    def body(kb, carry):
        step(kb, False)
        return carry

    lax.fori_loop(0, qi, body, 0)
    step(qi, True)
    for hd in range(heads):
        o_ref[:, MLA_V * hd:MLA_V * (hd + 1)] = (acc_ref[hd] / l_ref[hd]).T.astype(BF16)


def _mla_attention(q, k, vt, bsz, seq):
    tq = tk = min(MLA_TQ, seq)
    nq = seq // tq
    hp = MLA_HEADS_PER_STEP
    hw = MLA_QK_PAD
    return pl.pallas_call(
        functools.partial(_mla_attn_kernel, tq=tq, tk=tk, heads=hp),
        out_shape=jax.ShapeDtypeStruct((bsz * seq, MLA_HEADS * MLA_V), BF16),
        grid=(bsz, MLA_HEADS // hp, nq),
        in_specs=[pl.BlockSpec((tq, hp * hw), lambda b, h, i: (b * nq + i, h)),
                  pl.BlockSpec((seq, hp * hw), lambda b, h, i: (b, h)),
                  pl.BlockSpec((nq, hp * MLA_V, tk), lambda b, h, i: (b, h, 0))],
        out_specs=pl.BlockSpec((tq, hp * MLA_V), lambda b, h, i: (b * nq + i, h)),
        scratch_shapes=[pltpu.VMEM((hp, 1, tq), F32), pltpu.VMEM((hp, 1, tq), F32),
                        pltpu.VMEM((hp, MLA_V, tq), F32)],
        compiler_params=_cparams(("parallel", "parallel", "arbitrary"), 48),
        name="mla_attention",
    )(q, k, vt)


def _outproj_kernel(ys5_ref, yret_ref, yswa_ref, ymla_ref, h_ref, gt_ref, gw_ref, gb_ref, w_ref, o_ref,
                    *, per_seq):
    b = pl.program_id(0) // per_seq
    y = ys5_ref[...]
    z = jax.nn.gelu(y, approximate=True)
    gl = jnp.dot(z.astype(BF16), gw_ref[...], preferred_element_type=F32) + gb_ref[...]
    s5 = (z * jax.nn.sigmoid(gl)).astype(BF16)
    gw = GROUP_WIDTH
    mixed = jnp.dot(s5, w_ref[0:gw, :], preferred_element_type=F32)
    mixed += jnp.dot(yret_ref[...], w_ref[gw:2 * gw, :], preferred_element_type=F32)
    mixed += jnp.dot(yswa_ref[...], w_ref[2 * gw:3 * gw, :], preferred_element_type=F32)
    mixed += jnp.dot(ymla_ref[...], w_ref[3 * gw:4 * gw, :], preferred_element_type=F32)
    o_ref[...] = h_ref[...] + gt_ref[pl.ds(b, 1), :] * mixed


def _out_projection(ys5, yret, yswa, ymla, h2, mod, glu_w, glu_b, w_out, seq, layer):
    tokens, d = h2.shape
    tm = min(512, seq)
    per_seq = seq // tm
    gw = GROUP_WIDTH
    row = lambda w: pl.BlockSpec((tm, w), lambda i: (i, 0))
    return pl.pallas_call(
        functools.partial(_outproj_kernel, per_seq=per_seq),
        out_shape=jax.ShapeDtypeStruct((tokens, d), F32),
        grid=(tokens // tm,),
        in_specs=[row(gw), row(gw), row(gw), row(gw), row(d),
                  _mod_spec(layer, MOD_GT1, 1),
                  pl.BlockSpec((None, gw, gw), lambda i: (layer, 0, 0)),
                  _layer_vec(gw, layer, 1),
                  _resident((None, 4 * gw, d), lambda i: (layer, 0, 0))],
        out_specs=row(d),
        compiler_params=_cparams(("parallel",), 48),
        name="out_projection",
    )(ys5, yret, yswa, ymla, h2, mod, glu_w, glu_b, w_out)


def _mlp_kernel(h_ref, sc_ref, sh_ref, gt_ref, g_ref, fg_ref, w1_ref, w2_ref, o_ref, a_ref, *, per_seq, final):
    j = pl.program_id(1)
    b = pl.program_id(0) // per_seq

    @pl.when(j == 0)
    def _():
        _norm_modulate(h_ref, g_ref[...] * (1.0 + sc_ref[pl.ds(b, 1), :]), sh_ref[pl.ds(b, 1), :], a_ref)
        o_ref[...] = jnp.zeros_like(o_ref)

    hid = jnp.dot(a_ref[...], w1_ref[...], preferred_element_type=F32)
    hid = jnp.square(jnp.maximum(hid, 0.0)).astype(BF16)
    o_ref[...] += jnp.dot(hid, w2_ref[...], preferred_element_type=F32)

    @pl.when(j == pl.num_programs(1) - 1)
    def _():
        out = h_ref[...] + gt_ref[pl.ds(b, 1), :] * o_ref[...]
        if final:
            ms = jnp.mean(out * out, axis=-1, keepdims=True)
            out = out * lax.rsqrt(ms + EPS) * fg_ref[...]
        o_ref[...] = out


def _mlp(h2, mod, gains, final_gain, w1, w2, seq, layer, final):
    tokens, d = h2.shape
    dff = w1.shape[-1]
    tm = min(512, seq)
    tf = 1024
    per_seq = seq // tm
    return pl.pallas_call(
        functools.partial(_mlp_kernel, per_seq=per_seq, final=final),
        out_shape=jax.ShapeDtypeStruct((tokens, d), F32),
        grid=(tokens // tm, dff // tf),
        in_specs=[pl.BlockSpec((tm, d), lambda i, j: (i, 0)),
                  _mod_spec(layer, MOD_SC2, 2), _mod_spec(layer, MOD_SH2, 2), _mod_spec(layer, MOD_GT2, 2),
                  _layer_vec(d, layer, 2),
                  pl.BlockSpec((1, d), lambda i, j: (0, 0)),
                  pl.BlockSpec((None, d, tf), lambda i, j: (layer, 0, j)),
                  pl.BlockSpec((None, tf, d), lambda i, j: (layer, j, 0))],
        out_specs=pl.BlockSpec((tm, d), lambda i, j: (i, 0)),
        scratch_shapes=[pltpu.VMEM((tm, d), BF16)],
        compiler_params=_cparams(("parallel", "arbitrary"), 48),
        name="mlp",
    )(h2, mod, mod, mod, gains, final_gain, w1, w2)


def _swap_halves(w, heads, hd):
    lead = w.shape[:-1]
    w4 = w.reshape(lead + (heads, 2, hd // 2))
    return jnp.concatenate([w4[..., 1:, :], w4[..., :1, :]], axis=-2).reshape(lead + (heads * hd,))


def _mla_weight_ext(w_uq, w_ukv):
    hq = MLA_NOPE + MLA_ROPE
    hk = MLA_NOPE + MLA_V
    scale = hq ** -0.5 * math.log2(math.e)
    zq = lambda n: jnp.zeros(w_uq.shape[:-1] + (n,), w_uq.dtype)
    zk128 = jnp.zeros(w_ukv.shape[:-1] + (128,), w_ukv.dtype)
    plain, swapped, kcols, vcols = [], [], [], []
    for h in range(MLA_HEADS):
        plain += [w_uq[..., h * hq:(h + 1) * hq], zq(64)]
        swapped += [zq(128), _swap_halves(w_uq[..., h * hq + MLA_NOPE:(h + 1) * hq], 1, MLA_ROPE), zq(64)]
        kcols += [w_ukv[..., h * hk:h * hk + MLA_NOPE], zk128]
        vcols += [w_ukv[..., h * hk + MLA_NOPE:(h + 1) * hk]]
    wq_ext = (jnp.concatenate(plain + swapped, axis=-1) * scale).astype(BF16)
    wkv_ext = jnp.concatenate(kcols + vcols, axis=-1).astype(BF16)
    return wq_ext, wkv_ext


def _rotary_tables(seq):
    d = RET_QK
    inv = ROPE_BASE ** (-jnp.arange(0, d, 2, dtype=F32) / d)
    ang = jnp.arange(seq, dtype=F32)[:, None] * inv[None, :]
    cos, sin = jnp.cos(ang), jnp.sin(ang)
    cos1 = jnp.concatenate([cos, cos], axis=1)
    sin1 = jnp.concatenate([-sin, sin], axis=1)
    cos4, sin4 = jnp.tile(cos1, (1, 4)), jnp.tile(sin1, (1, 4))
    zeros64 = jnp.zeros((seq, 64), F32)
    cosq = jnp.concatenate([jnp.ones((seq, MLA_NOPE), F32), cos1, zeros64], axis=1)
    sinq = jnp.concatenate([jnp.zeros((seq, MLA_NOPE), F32), sin1, zeros64], axis=1)
    return cos4, sin4, cosq, sinq


def kernel(x, c, norm1_g, norm2_g, ada_w, ada_b, w_in, s5_lambda_re, s5_lambda_im, s5_log_dt, s5_b_re, s5_b_im, s5_c_re, s5_c_im, s5_d, s5_glu_w, s5_glu_b, swa_sinks, mla_q_norm, mla_kv_norm, mla_w_uq, mla_w_ukv, w_out, mlp_w1, mlp_w2, final_norm_g):
    bsz, seq, d = x.shape
    depth = ada_w.shape[0]
    tokens = bsz * seq
    h = x.reshape(tokens, d)
    mod = _adaln_mod(c, ada_w, ada_b)
    cos4, sin4, cosq, sinq = _rotary_tables(seq)
    w_ext = w_in.astype(BF16)
    wq_ext, wkv_ext = _mla_weight_ext(mla_w_uq, mla_w_ukv)
    w_out_b, glu_w_b = w_out.astype(BF16), s5_glu_w.astype(BF16)
    w1_b, w2_b = mlp_w1.astype(BF16), mlp_w2.astype(BF16)
    norm1 = norm1_g.reshape(depth, 1, d)
    norm2 = norm2_g.reshape(depth, 1, d)
    glu_b = s5_glu_b.reshape(depth, 1, -1)
    q_norm = mla_q_norm.reshape(depth, 1, -1)
    kv_norm = mla_kv_norm.reshape(depth, 1, -1)
    ncb = seq // S5_CHUNK
    s5_ops = _s5_prepare(s5_lambda_re, s5_lambda_im, s5_log_dt, s5_b_re, s5_b_im, s5_c_re, s5_c_im, s5_d,
                         S5_CHUNK, max(1, (ncb - 1).bit_length()))
    for l in range(depth):
        (u, rq, rk, rv, rg, wq, wk2, wv2, cq, ckv, kr_pad) = _in_projection(
            h, mod, norm1, w_ext, cos4, sin4, seq, l)
        y_s5 = _s5_mixer_pre_glu(u, s5_ops, bsz, seq, l)
        y_ret = _retention(rq, rk, rv, rg, bsz, seq)
        y_swa = _swa(wq, wk2, wv2, swa_sinks, bsz, seq, l)
        mq, mk, mvt = _mla_up(cq, ckv, kr_pad, q_norm, kv_norm, wq_ext, wkv_ext, cosq, sinq, seq, l)
        y_mla = _mla_attention(mq, mk, mvt, bsz, seq)
        h = _out_projection(y_s5, y_ret, y_swa, y_mla, h, mod, glu_w_b, glu_b, w_out_b, seq, l)
        h = _mlp(h, mod, norm2, final_norm_g.reshape(1, d), w1_b, w2_b, seq, l, final=(l == depth - 1))
    return h.reshape(bsz, seq, d)
```

```python
import functools
import math

import numpy as np
import jax
import jax.numpy as jnp
from jax import lax
from jax.experimental import pallas as pl
from jax.experimental.pallas import tpu as pltpu

F32 = jnp.float32
BF16 = jnp.bfloat16

D_MODEL = 2048
GROUP_WIDTH = 512
S5_CH = 16
S5_GROUPS = GROUP_WIDTH // S5_CH
S5_STATE = 64
RET_HEADS = 4
RET_QK = 64
RET_V = 128
SWA_HD = 64
SWA_HEADS = 8
SWA_KV_HEADS = 2
WINDOW = 128
MLA_HEADS = 4
MLA_Q_RANK = 384
MLA_KV_RANK = 128
MLA_NOPE = 128
MLA_ROPE = 64
MLA_V = 128
MLA_QK_PAD = 256
ROPE_BASE = 10000.0
EPS = 1e-6
NEG = -1e30

MIB = 1024 * 1024
LANES = 128
MOD_ROWS = 8
NORM_ROWS = 16
S5_CHUNK = 32
S5_GROUPS_PER_STEP = LANES // S5_CH
S5_BATCH_PER_STEP = 2
RET_BLOCK = 512
SWA_BLOCK = 512
MLA_TQ = 512
MLA_TK = 512
MLA_HEADS_PER_STEP = 4
MOD_SH1, MOD_SC1, MOD_GT1, MOD_SH2, MOD_SC2, MOD_GT2 = range(6)

_O_U, _O_RQ, _O_RK, _O_RV, _O_RG, _O_WQ, _O_WK, _O_WV, _O_CQ, _O_CKV, _O_KR, _O_END = (
    0, 512, 768, 1024, 1536, 2048, 2560, 2688, 2816, 3200, 3328, 3392)


def _cparams(sem, vmem_mib):
    return pltpu.CompilerParams(dimension_semantics=sem, vmem_limit_bytes=int(vmem_mib * MIB))


def _resident(block_shape, index_map):
    return pl.BlockSpec(block_shape, index_map, pipeline_mode=pl.Buffered(1))


def _mod_spec(layer, which, ngrid):
    if ngrid == 1:
        return pl.BlockSpec((None, None, MOD_ROWS, D_MODEL), lambda i: (layer, which, 0, 0))
    return pl.BlockSpec((None, None, MOD_ROWS, D_MODEL), lambda i, j: (layer, which, 0, 0))


def _norm_modulate(x_ref, gain, shift, out_ref):
    for r in range(0, x_ref.shape[0], NORM_ROWS):
        x = x_ref[r:r + NORM_ROWS, :]
        ms = jnp.mean(x * x, axis=-1, keepdims=True)
        out_ref[r:r + NORM_ROWS, :] = ((x * lax.rsqrt(ms + EPS)) * gain + shift).astype(out_ref.dtype)


def _layer_vec(width, layer, ngrid):
    if ngrid == 1:
        return pl.BlockSpec((None, 1, width), lambda i: (layer, 0, 0))
    return pl.BlockSpec((None, 1, width), lambda i, j: (layer, 0, 0))


def _mod_kernel(c_ref, w_ref, b_ref, o_ref):
    c = c_ref[...]
    ca = (c * jax.nn.sigmoid(c)).astype(BF16)
    o_ref[...] = jnp.dot(ca, w_ref[...].astype(BF16), preferred_element_type=F32) + b_ref[...]


def _adaln_mod(c, ada_w, ada_b):
    depth, d, n = ada_w.shape
    bsz = c.shape[0]
    cp = jnp.zeros((MOD_ROWS, d), F32).at[:bsz].set(c)
    tn = 1024
    per_vec = d // tn
    return pl.pallas_call(
        _mod_kernel,
        out_shape=jax.ShapeDtypeStruct((depth, n // d, MOD_ROWS, d), F32),
        grid=(depth, n // tn),
        in_specs=[
            pl.BlockSpec((MOD_ROWS, d), lambda l, j: (0, 0)),
            pl.BlockSpec((None, d, tn), lambda l, j: (l, 0, j)),
            pl.BlockSpec((None, 1, tn), lambda l, j: (l, 0, j)),
        ],
        out_specs=pl.BlockSpec((None, None, MOD_ROWS, tn), lambda l, j: (l, j // per_vec, 0, j % per_vec)),
        compiler_params=_cparams(("parallel", "parallel"), 40),
        name="adaln_mod",
    )(cp, ada_w, ada_b.reshape(depth, 1, n))


def _inproj_kernel(h_ref, sc_ref, sh_ref, g_ref, w_ref, cos_ref, sin_ref, swap_ref, krsel_ref, dup_ref,
                   u_ref, rq_ref, rk_ref, rv_ref, rg_ref, wq_ref, wk_ref, wv_ref,
                   cq_ref, ckv_ref, kr_ref, a_ref, *, per_seq):
    b = pl.program_id(0) // per_seq
    _norm_modulate(h_ref, g_ref[...] * (1.0 + sc_ref[pl.ds(b, 1), :]), sh_ref[pl.ds(b, 1), :], a_ref)
    cos = cos_ref[...]
    sin = sin_ref[...]
    swap = swap_ref[...]
    dup = dup_ref[...]

    def proj(lo, hi):
        return jnp.dot(a_ref[...], w_ref[:, lo:hi], preferred_element_type=F32)

    def rope(v):
        return v * cos + jnp.dot(v.astype(BF16), swap, preferred_element_type=F32) * sin

    p0 = proj(_O_U, _O_RV)
    p1 = proj(_O_RV, _O_WQ)
    u_ref[...] = p0[:, _O_U:_O_RQ]
    rq_ref[...] = rope(p0[:, _O_RQ:_O_RK]).astype(BF16)
    rk_ref[...] = rope(p0[:, _O_RK:_O_RV] * (RET_QK ** -0.5)).astype(BF16)
    p2 = proj(_O_WQ, _O_CQ)
    rv_ref[...] = p1[:, :_O_RG - _O_RV].astype(BF16)
    rg_ref[...] = p1[:, _O_RG - _O_RV:].astype(BF16)
    p3 = proj(_O_CQ, _O_END)
    wq_ref[...] = (p2[:, :_O_WK - _O_WQ] * (SWA_HD ** -0.5)).astype(BF16)
    wk_ref[...] = jnp.dot(p2[:, _O_WK - _O_WQ:_O_WV - _O_WQ].astype(BF16), dup,
                          preferred_element_type=F32).astype(BF16)
    wv_ref[...] = jnp.dot(p2[:, _O_WV - _O_WQ:].astype(BF16), dup, preferred_element_type=F32).astype(BF16)
    cq_ref[...] = p3[:, :_O_CKV - _O_CQ].astype(BF16)
    ckv_ref[...] = p3[:, _O_CKV - _O_CQ:_O_KR - _O_CQ].astype(BF16)
    kr2 = jnp.dot(p3[:, _O_KR - _O_CQ:].astype(BF16), krsel_ref[...], preferred_element_type=F32)
    kr = kr2[:, :128] * cos[:, :128] + kr2[:, 128:] * sin[:, :128]
    kr_ref[:, 0:128] = jnp.zeros_like(kr).astype(BF16)
    kr_ref[:, 128:256] = kr.astype(BF16)


def _lane_maps():
    half = RET_QK // 2
    j = np.arange(RET_HEADS * RET_QK)
    swap = np.zeros((j.size, j.size), np.float32)
    swap[(j // RET_QK) * RET_QK + (j % RET_QK + half) % RET_QK, j] = 1.0
    i = np.arange(MLA_ROPE)
    krsel = np.zeros((MLA_ROPE, 256), np.float32)
    krsel[i, i] = 1.0
    krsel[(i + half) % MLA_ROPE, 128 + i] = 1.0
    c = np.arange(256)
    dup = np.zeros((128, 256), np.float32)
    dup[(c // 128) * SWA_HD + c % SWA_HD, c] = 1.0
    return jnp.asarray(swap, BF16), jnp.asarray(krsel, BF16), jnp.asarray(dup, BF16)


def _in_projection(h2, mod, gains, w_in_b, cos4, sin4, seq, layer):
    tokens, d = h2.shape
    n_in = w_in_b.shape[-1]
    tm = 512
    per_seq = seq // tm
    widths = (512, 256, 256, 512, 512, 512, 256, 256, 384, 128, 256)
    dtypes = (F32,) + (BF16,) * 10
    row_blk = lambda w: pl.BlockSpec((tm, w), lambda i: (i, 0))
    const = lambda a: pl.BlockSpec(a.shape, lambda i: (0, 0))
    swap, krsel, dup = _lane_maps()
    return pl.pallas_call(
        functools.partial(_inproj_kernel, per_seq=per_seq),
        out_shape=tuple(jax.ShapeDtypeStruct((tokens, w), t) for w, t in zip(widths, dtypes)),
        grid=(tokens // tm,),
        in_specs=[
            row_blk(d),
            _mod_spec(layer, MOD_SC1, 1),
            _mod_spec(layer, MOD_SH1, 1),
            _layer_vec(d, layer, 1),
            _resident((None, d, n_in), lambda i: (layer, 0, 0)),
            pl.BlockSpec((tm, 256), lambda i: (i % per_seq, 0)),
            pl.BlockSpec((tm, 256), lambda i: (i % per_seq, 0)),
            const(swap), const(krsel), const(dup),
        ],
        out_specs=tuple(row_blk(w) for w in widths),
        scratch_shapes=[pltpu.VMEM((tm, d), BF16)],
        compiler_params=_cparams(("parallel",), 56),
        name="in_projection",
    )(h2, mod, mod, gains, w_in_b, cos4, sin4, swap, krsel, dup)


def _s5_prep_kernel(lr_ref, li_ref, ldt_ref, btr_ref, bti_ref, cr_ref, ci_ref, dlag0_ref,
                    a_ref, mint_ref, moutt_ref, pa_ref, pb_ref, *, chunk, nlev):
    width = chunk * S5_CH
    lr = lr_ref[...]
    li = li_ref[...]
    dt = jnp.exp(ldt_ref[...])
    btr, bti = btr_ref[...], bti_ref[...]
    cre, cim = cr_ref[...], ci_ref[...]

    def cpow(k):
        mag = jnp.exp(lr * dt * k)
        ang = li * dt * k
        return mag * jnp.cos(ang), mag * jnp.sin(ang)

    one = jnp.ones((1, 1), F32)
    ar, ai = cpow(one)
    den = lr * lr + li * li
    kr = ((ar - 1.0) * lr + ai * li) / den
    ki = (ai * lr - (ar - 1.0) * li) / den
    bbr = kr * btr - ki * bti
    bbi = kr * bti + ki * btr

    kidx = lax.broadcasted_iota(jnp.int32, (chunk, 1), 0).astype(F32)

    def outer(pr, pi, mr, mi):
        re = pr[:, None, :] * mr[None, :, :] - pi[:, None, :] * mi[None, :, :]
        im = pr[:, None, :] * mi[None, :, :] + pi[:, None, :] * mr[None, :, :]
        return re.reshape(width, S5_STATE), im.reshape(width, S5_STATE)

    er, ei = cpow((chunk - 1.0) - kidx)
    m_r, m_i = outer(er, ei, bbr, bbi)
    m_in = jnp.concatenate([m_r, m_i], axis=1)
    mint_ref[...] = m_in.T.astype(BF16)

    fr, fi = cpow(kidx + 1.0)
    w_r, w_i = outer(fr, fi, cre, cim)
    moutt_ref[...] = jnp.concatenate([w_r, -w_i], axis=1).astype(BF16)

    cc = jnp.concatenate([cre, -cim], axis=1)
    ks = lax.dot_general(cc, m_in, (((1,), (1,)), ((), ())), precision=lax.Precision.HIGHEST,
                         preferred_element_type=F32)
    ks = ks + dlag0_ref[...]
    lane = lax.broadcasted_iota(jnp.int32, (S5_CH, width), 1)
    for t in range(chunk):
        if t == chunk - 1:
            rows = ks
        else:
            shift = (width - S5_CH * (chunk - 1 - t)) % width
            rows = jnp.where(lane < S5_CH * (t + 1), pltpu.roll(ks, shift, axis=1), 0.0)
        a_ref[S5_CH * t:S5_CH * (t + 1), :] = rows.astype(BF16)

    lev = lax.broadcasted_iota(jnp.int32, (8, 1), 0)
    pr, pi = cpow((chunk * jnp.left_shift(1, jnp.minimum(lev, nlev))).astype(F32))
    reps = (2 * S5_STATE) // 8
    pa_ref[...] = jnp.tile(jnp.concatenate([pr, pr], axis=1), (reps, 1)).T
    pb_ref[...] = jnp.tile(jnp.concatenate([-pi, pi], axis=1), (reps, 1)).T


def _s5_prepare(lam_re, lam_im, log_dt, b_re, b_im, c_re, c_im, d_skip, chunk, nlev):
    p = lam_re.shape[-1]
    h = S5_CH
    g = lam_re.shape[0] * lam_re.shape[1]
    width = chunk * h
    d_lag0 = jnp.concatenate([jnp.zeros((g, h, width - h), F32),
                              d_skip.reshape(g, h, 1) * jnp.eye(h, dtype=F32)], axis=-1)
    p2 = 2 * p
    vec = lambda: pl.BlockSpec((None, 1, p), lambda i: (i, 0, 0))
    mat = lambda: pl.BlockSpec((None, h, p), lambda i: (i, 0, 0))
    out = lambda r, c: pl.BlockSpec((None, r, c), lambda i: (i, 0, 0))
    return pl.pallas_call(
        functools.partial(_s5_prep_kernel, chunk=chunk, nlev=nlev),
        out_shape=(
            jax.ShapeDtypeStruct((g, width, width), BF16),
            jax.ShapeDtypeStruct((g, p2, width), BF16),
            jax.ShapeDtypeStruct((g, width, p2), BF16),
            jax.ShapeDtypeStruct((g, p2, p2), F32),
            jax.ShapeDtypeStruct((g, p2, p2), F32),
        ),
        grid=(g,),
        in_specs=[vec(), vec(), vec(), mat(), mat(), mat(), mat(),
                  pl.BlockSpec((None, h, width), lambda i: (i, 0, 0))],
        out_specs=(out(width, width), out(p2, width), out(width, p2), out(p2, p2), out(p2, p2)),
        compiler_params=_cparams(("parallel",), 32),
        name="s5_prepare",
    )(lam_re.reshape(g, 1, p), lam_im.reshape(g, 1, p),
      jnp.broadcast_to(log_dt.reshape(g, 1, 1), (g, 1, p)),
      jnp.swapaxes(b_re, -1, -2).reshape(g, h, p), jnp.swapaxes(b_im, -1, -2).reshape(g, h, p),
      c_re.reshape(g, h, p), c_im.reshape(g, h, p), d_lag0)


def _s5_kernel(u_ref, a_ref, mint_ref, moutt_ref, pa_ref, pb_ref, eye_c_ref, eye_r_ref, y_ref, ut_ref, yt_ref,
               *, chunk, ncb, nlev):
    gps = S5_GROUPS_PER_STEP
    nchunks = ut_ref.shape[-1]
    nt = (((1,), (1,)), ((), ()))
    eye_c, eye_r = eye_c_ref[...], eye_r_ref[...]
    for s in range(chunk):
        slab = u_ref[pl.ds(s, nchunks, stride=chunk), :].astype(BF16)
        slab_t = lax.dot_general(eye_c, slab, nt, preferred_element_type=F32)
        for g in range(gps):
            ut_ref[g, S5_CH * s:S5_CH * (s + 1), :] = slab_t[S5_CH * g:S5_CH * (g + 1), :].astype(BF16)

    cidx = lax.broadcasted_iota(jnp.int32, (1, nchunks), 1) % ncb
    xs = []
    for g in range(gps):
        ut = ut_ref[g]
        yt_ref[g] = jnp.dot(a_ref[g], ut, preferred_element_type=F32)
        xs.append(jnp.dot(mint_ref[g], ut, preferred_element_type=F32))
    for k in range(nlev):
        d = 1 << k
        for g in range(gps):
            sh = jnp.where(cidx >= d, pltpu.roll(xs[g], d, axis=1), 0.0)
            shs = jnp.concatenate([sh[S5_STATE:], sh[:S5_STATE]], axis=0)
            xs[g] = xs[g] + pa_ref[g, :, k:k + 1] * sh + pb_ref[g, :, k:k + 1] * shs
    for g in range(gps):
        xp = jnp.where(cidx >= 1, pltpu.roll(xs[g], 1, axis=1), 0.0)
        yt_ref[g] += jnp.dot(moutt_ref[g], xp.astype(BF16), preferred_element_type=F32)

    for s in range(chunk):
        z = jnp.concatenate([yt_ref[g, S5_CH * s:S5_CH * (s + 1), :] for g in range(gps)], axis=0)
        z_hi = z.astype(BF16)
        z_lo = (z - z_hi.astype(F32)).astype(BF16)
        y_ref[pl.ds(s, nchunks, stride=chunk), :] = (
            lax.dot_general(eye_r, z_hi, nt, preferred_element_type=F32)
            + lax.dot_general(eye_r, z_lo, nt, preferred_element_type=F32))


def _s5_mixer_pre_glu(u, ops, bsz, seq, layer):
    a_op, mint, moutt, pa, pb = ops
    chunk = S5_CHUNK
    ncb = seq // chunk
    nlev = max(1, (ncb - 1).bit_length())
    nb = min(S5_BATCH_PER_STEP, bsz)
    gps = S5_GROUPS_PER_STEP
    nsg = S5_GROUPS // gps
    width = chunk * S5_CH
    p2 = 2 * S5_STATE
    nchunks = nb * ncb
    grp = lambda r, c: pl.BlockSpec((gps, r, c), lambda sg, bp: (layer * nsg + sg, 0, 0))
    tok = pl.BlockSpec((nb * seq, LANES), lambda sg, bp: (bp, sg))
    return pl.pallas_call(
        functools.partial(_s5_kernel, chunk=chunk, ncb=ncb, nlev=nlev),
        out_shape=jax.ShapeDtypeStruct((bsz * seq, GROUP_WIDTH), F32),
        grid=(nsg, bsz // nb),
        in_specs=[tok, grp(width, width), grp(p2, width), grp(width, p2), grp(p2, p2), grp(p2, p2),
                  pl.BlockSpec((LANES, LANES), lambda sg, bp: (0, 0)),
                  pl.BlockSpec((nchunks, nchunks), lambda sg, bp: (0, 0))],
        out_specs=tok,
        scratch_shapes=[pltpu.VMEM((gps, width, nchunks), BF16), pltpu.VMEM((gps, width, nchunks), F32)],
        compiler_params=_cparams(("parallel", "parallel"), 48),
        name="s5_mixer",
    )(u, a_op, mint, moutt, pa, pb, jnp.eye(LANES, dtype=BF16), jnp.eye(nchunks, dtype=BF16))


def _ret_kernel(q_ref, k_ref, v_ref, g_ref, dec_ref, xi_ref, zeta_ref, gc_ref, bd_ref, o_ref, st_ref):
    @pl.when(pl.program_id(1) == 0)
    def _():
        st_ref[...] = jnp.zeros_like(st_ref)

    q = q_ref[...]
    k = k_ref[...]
    v = v_ref[...]
    lane_head = lax.broadcasted_iota(jnp.int32, (1, RET_HEADS * RET_QK), 1) // RET_QK
    st = st_ref[...]
    o_cross = jnp.dot(q, st.astype(BF16), preferred_element_type=F32) * xi_ref[...]
    scores = []
    for hd in range(RET_HEADS):
        km = jnp.where(lane_head == hd, k, jnp.zeros_like(k))
        scores.append(lax.dot_general(q, km, (((1,), (1,)), ((), ())), preferred_element_type=F32))
    probs = [(scores[hd] * dec_ref[hd]).astype(BF16) for hd in range(RET_HEADS)]
    outs = []
    for hd in range(RET_HEADS):
        vh = v[:, hd * RET_V:(hd + 1) * RET_V]
        o = jnp.dot(probs[hd], vh, preferred_element_type=F32) + o_cross[:, hd * RET_V:(hd + 1) * RET_V]
        outs.append(o * lax.rsqrt(jnp.mean(o * o, axis=-1, keepdims=True) + EPS))
    on = jnp.concatenate(outs, axis=1)
    gate = g_ref[...].astype(F32)
    o_ref[...] = (on * (gate * jax.nn.sigmoid(gate))).astype(BF16)

    kz = (k.astype(F32) * zeta_ref[...]).astype(BF16)
    kv = lax.dot_general(kz, v, (((0,), (0,)), ((), ())), preferred_element_type=F32)
    st_ref[...] = st * gc_ref[...] + kv * bd_ref[...]


def _retention(rq, rk, rv, rg, bsz, seq):
    cb = min(RET_BLOCK, seq)
    nblk = seq // cb
    hq = RET_HEADS * RET_QK
    log_gamma = jnp.log1p(-(2.0 ** (-5.0 - jnp.arange(RET_HEADS, dtype=F32))))
    idx = jnp.arange(cb, dtype=F32)
    rel = idx[:, None] - idx[None, :]
    decay = jnp.where(rel >= 0, jnp.exp(log_gamma[:, None, None] * jnp.maximum(rel, 0.0)), 0.0)
    xi = jnp.repeat(jnp.exp(log_gamma[None, :] * (idx[:, None] + 1.0)), RET_V, axis=1)
    zeta = jnp.repeat(jnp.exp(log_gamma[None, :] * (cb - 1.0 - idx[:, None])), RET_QK, axis=1)
    gchunk = jnp.repeat(jnp.exp(log_gamma * cb), RET_V)[None, :]
    blockdiag = (jnp.arange(hq)[:, None] // RET_QK == jnp.arange(GROUP_WIDTH)[None, :] // RET_V).astype(F32)
    tok = lambda w: pl.BlockSpec((cb, w), lambda b, c: (b * nblk + c, 0))
    const = lambda shape: pl.BlockSpec(shape, lambda b, c: (0,) * len(shape))
    return pl.pallas_call(
        _ret_kernel,
        out_shape=jax.ShapeDtypeStruct((bsz * seq, GROUP_WIDTH), BF16),
        grid=(bsz, nblk),
        in_specs=[tok(hq), tok(hq), tok(GROUP_WIDTH), tok(GROUP_WIDTH),
                  const((RET_HEADS, cb, cb)), const((cb, GROUP_WIDTH)), const((cb, hq)),
                  const((1, GROUP_WIDTH)), const((hq, GROUP_WIDTH))],
        out_specs=tok(GROUP_WIDTH),
        scratch_shapes=[pltpu.VMEM((hq, GROUP_WIDTH), F32)],
        compiler_params=_cparams(("parallel", "arbitrary"), 32),
        name="retention",
    )(rq, rk, rv, rg, decay, xi, zeta, gchunk, blockdiag)


def _swa_kernel(sink_ref, q_ref, kc_ref, kp_ref, vc_ref, vp_ref, o_ref, *, nsub):
    blk = pl.program_id(1)
    kall = jnp.concatenate([kp_ref[...], kc_ref[...]], axis=0)
    vall = jnp.concatenate([vp_ref[...], vc_ref[...]], axis=0)
    half = lax.broadcasted_iota(jnp.int32, (1, 2 * SWA_HD), 1) // SWA_HD
    r = lax.broadcasted_iota(jnp.int32, (WINDOW, 2 * WINDOW), 0)
    jc = lax.broadcasted_iota(jnp.int32, (WINDOW, 2 * WINDOW), 1)
    dist = r + WINDOW - jc
    band = jnp.logical_and(dist >= 0, dist < WINDOW)
    first_key = jnp.where(blk > 0, 0, WINDOW)
    band_first = jnp.logical_and(band, jc >= first_key)
    items = [(sb, j) for sb in range(nsub) for j in range(SWA_KV_HEADS)]
    scores, probs, invs_all = [], [], []
    for sb, j in items:
        r0 = sb * WINDOW
        kk = kall[r0:r0 + 2 * WINDOW, 128 * j:128 * (j + 1)]
        zk = jnp.zeros_like(kk)
        kcat = jnp.concatenate([jnp.where(half == 0, kk, zk), jnp.where(half == 1, kk, zk)], axis=0)
        q2 = jnp.concatenate([q_ref[r0:r0 + WINDOW, 256 * j:256 * j + 128],
                              q_ref[r0:r0 + WINDOW, 256 * j + 128:256 * j + 256]], axis=0)
        scores.append(lax.dot_general(q2, kcat, (((1,), (1,)), ((), ())), preferred_element_type=F32))
    for (sb, j), s_all in zip(items, scores):
        valid = band_first if sb == 0 else band
        p_rows, inv_rows = [], []
        for a in range(2):
            p_cols, invs = [], []
            for e in range(2):
                sink = sink_ref[4 * j + 2 * a + e]
                s = s_all[WINDOW * a:WINDOW * (a + 1), 2 * WINDOW * e:2 * WINDOW * (e + 1)]
                s = jnp.where(valid, s, NEG)
                m = jnp.maximum(jnp.max(s, axis=-1, keepdims=True), sink)
                p = jnp.exp(s - m)
                denom = jnp.sum(p, axis=-1, keepdims=True) + jnp.exp(sink - m)
                p_cols.append(p.astype(BF16))
                invs.append(1.0 / denom)
            p_rows.append(jnp.concatenate(p_cols, axis=1))
            inv_rows.append(jnp.where(half == 0, invs[0], invs[1]))
        probs.append(jnp.concatenate(p_rows, axis=0))
        invs_all.append(inv_rows)
    for (sb, j), p_all, inv_rows in zip(items, probs, invs_all):
        r0 = sb * WINDOW
        vv = vall[r0:r0 + 2 * WINDOW, 128 * j:128 * (j + 1)]
        zv = jnp.zeros_like(vv)
        vcat = jnp.concatenate([jnp.where(half == 0, vv, zv), jnp.where(half == 1, vv, zv)], axis=0)
        o = jnp.dot(p_all, vcat, preferred_element_type=F32)
        for a in range(2):
            o_ref[r0:r0 + WINDOW, 256 * j + 128 * a:256 * j + 128 * (a + 1)] = (
                o[WINDOW * a:WINDOW * (a + 1)] * inv_rows[a]).astype(BF16)


def _swa(wq, wk2, wv2, sinks, bsz, seq, layer):
    qb = min(SWA_BLOCK, seq)
    nblk = seq // qb
    per = qb // WINDOW
    cur = lambda w: pl.BlockSpec((qb, w), lambda b, i: (b * nblk + i, 0))
    prev = lambda w: pl.BlockSpec((WINDOW, w), lambda b, i: (jnp.maximum((b * nblk + i) * per - 1, 0), 0))
    return pl.pallas_call(
        functools.partial(_swa_kernel, nsub=per),
        out_shape=jax.ShapeDtypeStruct((bsz * seq, GROUP_WIDTH), BF16),
        grid=(bsz, nblk),
        in_specs=[pl.BlockSpec(memory_space=pltpu.SMEM),
                  cur(GROUP_WIDTH), cur(256), prev(256), cur(256), prev(256)],
        out_specs=cur(GROUP_WIDTH),
        compiler_params=_cparams(("parallel", "parallel"), 32),
        name="swa",
    )(sinks[layer], wq, wk2, wk2, wv2, wv2)


def _mla_up_kernel(cq_ref, ckv_ref, kr_ref, qn_ref, kvn_ref, wq_ref, wkv_ref, cos_ref, sin_ref,
                   q_ref, k_ref, vt_ref):
    cq = cq_ref[...].astype(F32)
    nq = (cq * lax.rsqrt(jnp.mean(cq * cq, axis=-1, keepdims=True) + EPS) * qn_ref[...]).astype(BF16)
    qq = jnp.dot(nq, wq_ref[...], preferred_element_type=F32)
    ckv = ckv_ref[...].astype(F32)
    nkv = (ckv * lax.rsqrt(jnp.mean(ckv * ckv, axis=-1, keepdims=True) + EPS) * kvn_ref[...]).astype(BF16)
    kv = jnp.dot(nkv, wkv_ref[...], preferred_element_type=F32)
    cos, sin = cos_ref[...], sin_ref[...]
    kr = kr_ref[...].astype(F32)
    hw = MLA_QK_PAD
    for hd in range(MLA_HEADS):
        q_ref[:, hw * hd:hw * (hd + 1)] = (
            qq[:, hw * hd:hw * (hd + 1)] * cos
            + qq[:, hw * (MLA_HEADS + hd):hw * (MLA_HEADS + hd + 1)] * sin).astype(BF16)
        k_ref[:, hw * hd:hw * (hd + 1)] = (kv[:, hw * hd:hw * (hd + 1)] + kr).astype(BF16)
    vt_ref[...] = kv[:, hw * MLA_HEADS:].T.astype(BF16)


def _mla_up(cq, ckv, kr_pad, q_norm, kv_norm, wq_ext, wkv_ext, cosq, sinq, seq, layer):
    tokens = cq.shape[0]
    tm = min(MLA_TK, seq)
    per_seq = seq // tm
    hw = MLA_QK_PAD
    row = lambda w: pl.BlockSpec((tm, w), lambda i: (i, 0))
    tab = lambda: pl.BlockSpec((tm, hw), lambda i: (i % per_seq, 0))
    lay = lambda a: pl.BlockSpec((None,) + a.shape[1:], lambda i: (layer, 0, 0))
    return pl.pallas_call(
        _mla_up_kernel,
        out_shape=(jax.ShapeDtypeStruct((tokens, MLA_HEADS * hw), BF16),
                   jax.ShapeDtypeStruct((tokens, MLA_HEADS * hw), BF16),
                   jax.ShapeDtypeStruct((tokens // tm, MLA_HEADS * MLA_V, tm), BF16)),
        grid=(tokens // tm,),
        in_specs=[row(MLA_Q_RANK), row(MLA_KV_RANK), row(hw), lay(q_norm), lay(kv_norm),
                  lay(wq_ext), lay(wkv_ext), tab(), tab()],
        out_specs=(row(MLA_HEADS * hw), row(MLA_HEADS * hw),
                   pl.BlockSpec((None, MLA_HEADS * MLA_V, tm), lambda i: (i, 0, 0))),
        compiler_params=_cparams(("parallel",), 40),
        name="mla_up",
    )(cq, ckv, kr_pad, q_norm, kv_norm, wq_ext, wkv_ext, cosq, sinq)


def _mla_attn_kernel(q_ref, k_ref, vt_ref, o_ref, m_ref, l_ref, acc_ref, *, tq, tk, heads):
    qi = pl.program_id(2)
    hw = MLA_QK_PAD
    m_ref[...] = jnp.full_like(m_ref, NEG)
    l_ref[...] = jnp.zeros_like(l_ref)
    acc_ref[...] = jnp.zeros_like(acc_ref)

    def step(kbs, masked):
        sts, ps, alphas = [], [], []
        for hd in range(heads):
            q = q_ref[:, hw * hd:hw * (hd + 1)]
            blocks = []
            for kb in kbs:
                ks = k_ref[pl.ds(pl.multiple_of(kb * tk, tk), tk), hw * hd:hw * (hd + 1)]
                st = lax.dot_general(ks, q, (((1,), (1,)), ((), ())), preferred_element_type=F32)
                if masked:
                    key = lax.broadcasted_iota(jnp.int32, (tk, tq), 0)
                    qry = lax.broadcasted_iota(jnp.int32, (tk, tq), 1)
                    st = jnp.where(key <= qry, st, NEG)
                blocks.append(st)
            sts.append(blocks)
        for hd in range(heads):
            m_old = m_ref[hd]
            m_new = m_old
            for st in sts[hd]:
                m_new = jnp.maximum(m_new, jnp.max(st, axis=0, keepdims=True))
            alpha = jnp.exp2(m_old - m_new)
            l_new = alpha * l_ref[hd]
            blocks = []
            for st in sts[hd]:
                p = jnp.exp2(st - m_new)
                l_new = l_new + jnp.sum(p, axis=0, keepdims=True)
                blocks.append(p.astype(BF16))
            l_ref[hd] = l_new
            m_ref[hd] = m_new
            ps.append(blocks)
            alphas.append(alpha)
        for hd in range(heads):
            acc = alphas[hd] * acc_ref[hd]
            for kb, p in zip(kbs, ps[hd]):
                acc = acc + jnp.dot(vt_ref[kb, MLA_V * hd:MLA_V * (hd + 1), :], p, preferred_element_type=F32)
            acc_ref[hd] = acc

    def body(pair, carry):
        step((2 * pair, 2 * pair + 1), False)
        return carry

    lax.fori_loop(0, qi // 2, body, 0)

    @pl.when(qi % 2 == 1)
    def _():
        step((qi - 1,), False)

    step((qi,), True)
    for hd in range(heads):
        o_ref[:, MLA_V * hd:MLA_V * (hd + 1)] = (acc_ref[hd] / l_ref[hd]).T.astype(BF16)


def _mla_attention(q, k, vt, bsz, seq):
    tq = tk = min(MLA_TQ, seq)
    nq = seq // tq
    hp = MLA_HEADS_PER_STEP
    hw = MLA_QK_PAD
    return pl.pallas_call(
        functools.partial(_mla_attn_kernel, tq=tq, tk=tk, heads=hp),
        out_shape=jax.ShapeDtypeStruct((bsz * seq, MLA_HEADS * MLA_V), BF16),
        grid=(bsz, MLA_HEADS // hp, nq),
        in_specs=[pl.BlockSpec((tq, hp * hw), lambda b, h, i: (b * nq + i, h)),
                  pl.BlockSpec((seq, hp * hw), lambda b, h, i: (b, h)),
                  pl.BlockSpec((nq, hp * MLA_V, tk), lambda b, h, i: (b, h, 0))],
        out_specs=pl.BlockSpec((tq, hp * MLA_V), lambda b, h, i: (b * nq + i, h)),
        scratch_shapes=[pltpu.VMEM((hp, 1, tq), F32), pltpu.VMEM((hp, 1, tq), F32),
                        pltpu.VMEM((hp, MLA_V, tq), F32)],
        compiler_params=_cparams(("parallel", "parallel", "arbitrary"), 48),
        name="mla_attention",
    )(q, k, vt)


def _outproj_kernel(ys5_ref, yret_ref, yswa_ref, ymla_ref, h_ref, gt_ref, gw_ref, gb_ref, w_ref, o_ref,
                    *, per_seq):
    b = pl.program_id(0) // per_seq
    y = ys5_ref[...]
    z = jax.nn.gelu(y, approximate=True)
    gl = jnp.dot(z.astype(BF16), gw_ref[...], preferred_element_type=F32) + gb_ref[...]
    s5 = (z * jax.nn.sigmoid(gl)).astype(BF16)
    gw = GROUP_WIDTH
    mixed = jnp.dot(s5, w_ref[0:gw, :], preferred_element_type=F32)
    mixed += jnp.dot(yret_ref[...], w_ref[gw:2 * gw, :], preferred_element_type=F32)
    mixed += jnp.dot(yswa_ref[...], w_ref[2 * gw:3 * gw, :], preferred_element_type=F32)
    mixed += jnp.dot(ymla_ref[...], w_ref[3 * gw:4 * gw, :], preferred_element_type=F32)
    o_ref[...] = h_ref[...] + gt_ref[pl.ds(b, 1), :] * mixed


def _out_projection(ys5, yret, yswa, ymla, h2, mod, glu_w, glu_b, w_out, seq, layer):
    tokens, d = h2.shape
    tm = min(512, seq)
    per_seq = seq // tm
    gw = GROUP_WIDTH
    row = lambda w: pl.BlockSpec((tm, w), lambda i: (i, 0))
    return pl.pallas_call(
        functools.partial(_outproj_kernel, per_seq=per_seq),
        out_shape=jax.ShapeDtypeStruct((tokens, d), F32),
        grid=(tokens // tm,),
        in_specs=[row(gw), row(gw), row(gw), row(gw), row(d),
                  _mod_spec(layer, MOD_GT1, 1),
                  pl.BlockSpec((None, gw, gw), lambda i: (layer, 0, 0)),
                  _layer_vec(gw, layer, 1),
                  _resident((None, 4 * gw, d), lambda i: (layer, 0, 0))],
        out_specs=row(d),
        compiler_params=_cparams(("parallel",), 48),
        name="out_projection",
    )(ys5, yret, yswa, ymla, h2, mod, glu_w, glu_b, w_out)


def _mlp_kernel(h_ref, sc_ref, sh_ref, gt_ref, g_ref, fg_ref, w1_ref, w2_ref, o_ref, a_ref, *, per_seq, final):
    j = pl.program_id(1)
    b = pl.program_id(0) // per_seq

    @pl.when(j == 0)
    def _():
        _norm_modulate(h_ref, g_ref[...] * (1.0 + sc_ref[pl.ds(b, 1), :]), sh_ref[pl.ds(b, 1), :], a_ref)
        o_ref[...] = jnp.zeros_like(o_ref)

    hid = jnp.dot(a_ref[...], w1_ref[...], preferred_element_type=F32)
    hid = jnp.square(jnp.maximum(hid, 0.0)).astype(BF16)
    o_ref[...] += jnp.dot(hid, w2_ref[...], preferred_element_type=F32)

    @pl.when(j == pl.num_programs(1) - 1)
    def _():
        out = h_ref[...] + gt_ref[pl.ds(b, 1), :] * o_ref[...]
        if final:
            ms = jnp.mean(out * out, axis=-1, keepdims=True)
            out = out * lax.rsqrt(ms + EPS) * fg_ref[...]
        o_ref[...] = out


def _mlp(h2, mod, gains, final_gain, w1, w2, seq, layer, final):
    tokens, d = h2.shape
    dff = w1.shape[-1]
    tm = min(512, seq)
    tf = 1024
    per_seq = seq // tm
    return pl.pallas_call(
        functools.partial(_mlp_kernel, per_seq=per_seq, final=final),
        out_shape=jax.ShapeDtypeStruct((tokens, d), F32),
        grid=(tokens // tm, dff // tf),
        in_specs=[pl.BlockSpec((tm, d), lambda i, j: (i, 0)),
                  _mod_spec(layer, MOD_SC2, 2), _mod_spec(layer, MOD_SH2, 2), _mod_spec(layer, MOD_GT2, 2),
                  _layer_vec(d, layer, 2),
                  pl.BlockSpec((1, d), lambda i, j: (0, 0)),
                  pl.BlockSpec((None, d, tf), lambda i, j: (layer, 0, j)),
                  pl.BlockSpec((None, tf, d), lambda i, j: (layer, j, 0))],
        out_specs=pl.BlockSpec((tm, d), lambda i, j: (i, 0)),
        scratch_shapes=[pltpu.VMEM((tm, d), BF16)],
        compiler_params=_cparams(("parallel", "arbitrary"), 48),
        name="mlp",
    )(h2, mod, mod, mod, gains, final_gain, w1, w2)


def _swap_halves(w, heads, hd):
    lead = w.shape[:-1]
    w4 = w.reshape(lead + (heads, 2, hd // 2))
    return jnp.concatenate([w4[..., 1:, :], w4[..., :1, :]], axis=-2).reshape(lead + (heads * hd,))


def _mla_weight_ext(w_uq, w_ukv):
    hq = MLA_NOPE + MLA_ROPE
    hk = MLA_NOPE + MLA_V
    scale = hq ** -0.5 * math.log2(math.e)
    zq = lambda n: jnp.zeros(w_uq.shape[:-1] + (n,), w_uq.dtype)
    zk128 = jnp.zeros(w_ukv.shape[:-1] + (128,), w_ukv.dtype)
    plain, swapped, kcols, vcols = [], [], [], []
    for h in range(MLA_HEADS):
        plain += [w_uq[..., h * hq:(h + 1) * hq], zq(64)]
        swapped += [zq(128), _swap_halves(w_uq[..., h * hq + MLA_NOPE:(h + 1) * hq], 1, MLA_ROPE), zq(64)]
        kcols += [w_ukv[..., h * hk:h * hk + MLA_NOPE], zk128]
        vcols += [w_ukv[..., h * hk + MLA_NOPE:(h + 1) * hk]]
    wq_ext = (jnp.concatenate(plain + swapped, axis=-1) * scale).astype(BF16)
    wkv_ext = jnp.concatenate(kcols + vcols, axis=-1).astype(BF16)
    return wq_ext, wkv_ext


def _rotary_tables(seq):
    d = RET_QK
    inv = ROPE_BASE ** (-jnp.arange(0, d, 2, dtype=F32) / d)
    ang = jnp.arange(seq, dtype=F32)[:, None] * inv[None, :]
    cos, sin = jnp.cos(ang), jnp.sin(ang)
    cos1 = jnp.concatenate([cos, cos], axis=1)
    sin1 = jnp.concatenate([-sin, sin], axis=1)
    cos4, sin4 = jnp.tile(cos1, (1, 4)), jnp.tile(sin1, (1, 4))
    zeros64 = jnp.zeros((seq, 64), F32)
    cosq = jnp.concatenate([jnp.ones((seq, MLA_NOPE), F32), cos1, zeros64], axis=1)
    sinq = jnp.concatenate([jnp.zeros((seq, MLA_NOPE), F32), sin1, zeros64], axis=1)
    return cos4, sin4, cosq, sinq


def kernel(x, c, norm1_g, norm2_g, ada_w, ada_b, w_in, s5_lambda_re, s5_lambda_im, s5_log_dt, s5_b_re, s5_b_im, s5_c_re, s5_c_im, s5_d, s5_glu_w, s5_glu_b, swa_sinks, mla_q_norm, mla_kv_norm, mla_w_uq, mla_w_ukv, w_out, mlp_w1, mlp_w2, final_norm_g):
    bsz, seq, d = x.shape
    depth = ada_w.shape[0]
    tokens = bsz * seq
    h = x.reshape(tokens, d)
    mod = _adaln_mod(c, ada_w, ada_b)
    cos4, sin4, cosq, sinq = _rotary_tables(seq)
    w_ext = w_in.astype(BF16)
    wq_ext, wkv_ext = _mla_weight_ext(mla_w_uq, mla_w_ukv)
    w_out_b, glu_w_b = w_out.astype(BF16), s5_glu_w.astype(BF16)
    w1_b, w2_b = mlp_w1.astype(BF16), mlp_w2.astype(BF16)
    norm1 = norm1_g.reshape(depth, 1, d)
    norm2 = norm2_g.reshape(depth, 1, d)
    glu_b = s5_glu_b.reshape(depth, 1, -1)
    q_norm = mla_q_norm.reshape(depth, 1, -1)
    kv_norm = mla_kv_norm.reshape(depth, 1, -1)
    ncb = seq // S5_CHUNK
    s5_ops = _s5_prepare(s5_lambda_re, s5_lambda_im, s5_log_dt, s5_b_re, s5_b_im, s5_c_re, s5_c_im, s5_d,
                         S5_CHUNK, max(1, (ncb - 1).bit_length()))
    for l in range(depth):
        (u, rq, rk, rv, rg, wq, wk2, wv2, cq, ckv, kr_pad) = _in_projection(
            h, mod, norm1, w_ext, cos4, sin4, seq, l)
        y_s5 = _s5_mixer_pre_glu(u, s5_ops, bsz, seq, l)
        y_ret = _retention(rq, rk, rv, rg, bsz, seq)
        y_swa = _swa(wq, wk2, wv2, swa_sinks, bsz, seq, l)
        mq, mk, mvt = _mla_up(cq, ckv, kr_pad, q_norm, kv_norm, wq_ext, wkv_ext, cosq, sinq, seq, l)
        y_mla = _mla_attention(mq, mk, mvt, bsz, seq)
        h = _out_projection(y_s5, y_ret, y_swa, y_mla, h, mod, glu_w_b, glu_b, w_out_b, seq, l)
        h = _mlp(h, mod, norm2, final_norm_g.reshape(1, d), w1_b, w2_b, seq, l, final=(l == depth - 1))
    return h.reshape(bsz, seq, d)
```

```python
import functools
import math

import numpy as np
import jax
import jax.numpy as jnp
from jax import lax
from jax.experimental import pallas as pl
from jax.experimental.pallas import tpu as pltpu

F32 = jnp.float32
BF16 = jnp.bfloat16

D_MODEL = 2048
GROUP_WIDTH = 512
S5_CH = 16
S5_GROUPS = GROUP_WIDTH // S5_CH
S5_STATE = 64
RET_HEADS = 4
RET_QK = 64
RET_V = 128
SWA_HD = 64
SWA_HEADS = 8
SWA_KV_HEADS = 2
WINDOW = 128
MLA_HEADS = 4
MLA_Q_RANK = 384
MLA_KV_RANK = 128
MLA_NOPE = 128
MLA_ROPE = 64
MLA_V = 128
MLA_QK_PAD = 256
ROPE_BASE = 10000.0
EPS = 1e-6
NEG = -1e30

MIB = 1024 * 1024
LANES = 128
MOD_ROWS = 8
NORM_ROWS = 16
S5_CHUNK = 32
S5_GROUPS_PER_STEP = LANES // S5_CH
S5_BATCH_PER_STEP = 2
RET_BLOCK = 512
SWA_BLOCK = 512
MLA_TQ = 512
MLA_TK = 512
MLA_HEADS_PER_STEP = 4
MOD_SH1, MOD_SC1, MOD_GT1, MOD_SH2, MOD_SC2, MOD_GT2 = range(6)

_O_U, _O_RQ, _O_RK, _O_RV, _O_RG, _O_WQ, _O_WK, _O_WV, _O_CQ, _O_CKV, _O_KR, _O_END = (
    0, 512, 768, 1024, 1536, 2048, 2560, 2688, 2816, 3200, 3328, 3392)


def _cparams(sem, vmem_mib):
    return pltpu.CompilerParams(dimension_semantics=sem, vmem_limit_bytes=int(vmem_mib * MIB))


def _resident(block_shape, index_map):
    return pl.BlockSpec(block_shape, index_map, pipeline_mode=pl.Buffered(1))


def _mod_spec(layer, which, ngrid):
    if ngrid == 1:
        return pl.BlockSpec((None, None, MOD_ROWS, D_MODEL), lambda i: (layer, which, 0, 0))
    return pl.BlockSpec((None, None, MOD_ROWS, D_MODEL), lambda i, j: (layer, which, 0, 0))


def _norm_modulate(x_ref, gain, shift, out_ref):
    for r in range(0, x_ref.shape[0], NORM_ROWS):
        x = x_ref[r:r + NORM_ROWS, :]
        ms = jnp.mean(x * x, axis=-1, keepdims=True)
        out_ref[r:r + NORM_ROWS, :] = ((x * lax.rsqrt(ms + EPS)) * gain + shift).astype(out_ref.dtype)


def _layer_vec(width, layer, ngrid):
    if ngrid == 1:
        return pl.BlockSpec((None, 1, width), lambda i: (layer, 0, 0))
    return pl.BlockSpec((None, 1, width), lambda i, j: (layer, 0, 0))


def _mod_kernel(c_ref, w_ref, b_ref, o_ref):
    c = c_ref[...]
    ca = (c * jax.nn.sigmoid(c)).astype(BF16)
    o_ref[...] = jnp.dot(ca, w_ref[...].astype(BF16), preferred_element_type=F32) + b_ref[...]


def _adaln_mod(c, ada_w, ada_b):
    depth, d, n = ada_w.shape
    bsz = c.shape[0]
    cp = jnp.zeros((MOD_ROWS, d), F32).at[:bsz].set(c)
    tn = 1024
    per_vec = d // tn
    return pl.pallas_call(
        _mod_kernel,
        out_shape=jax.ShapeDtypeStruct((depth, n // d, MOD_ROWS, d), F32),
        grid=(depth, n // tn),
        in_specs=[
            pl.BlockSpec((MOD_ROWS, d), lambda l, j: (0, 0)),
            pl.BlockSpec((None, d, tn), lambda l, j: (l, 0, j)),
            pl.BlockSpec((None, 1, tn), lambda l, j: (l, 0, j)),
        ],
        out_specs=pl.BlockSpec((None, None, MOD_ROWS, tn), lambda l, j: (l, j // per_vec, 0, j % per_vec)),
        compiler_params=_cparams(("parallel", "parallel"), 40),
        name="adaln_mod",
    )(cp, ada_w, ada_b.reshape(depth, 1, n))


def _inproj_kernel(h_ref, sc_ref, sh_ref, g_ref, w_ref, wkr_ref, cos_ref, sin_ref, swap_ref, krsel_ref, dup_ref,
                   u_ref, rq_ref, rk_ref, rv_ref, rg_ref, wq_ref, wk_ref, wv_ref,
                   cq_ref, ckv_ref, kr_ref, a_ref, *, per_seq):
    b = pl.program_id(0) // per_seq
    _norm_modulate(h_ref, g_ref[...] * (1.0 + sc_ref[pl.ds(b, 1), :]), sh_ref[pl.ds(b, 1), :], a_ref)
    cos = cos_ref[...]
    sin = sin_ref[...]
    swap = swap_ref[...]
    dup = dup_ref[...]

    def proj(lo, hi):
        return jnp.dot(a_ref[...], w_ref[:, lo:hi], preferred_element_type=F32)

    def rope(v):
        return v * cos + jnp.dot(v.astype(BF16), swap, preferred_element_type=F32) * sin

    p0 = proj(_O_U, _O_RV)
    p1 = proj(_O_RV, _O_WQ)
    u_ref[...] = p0[:, _O_U:_O_RQ]
    rq_ref[...] = rope(p0[:, _O_RQ:_O_RK]).astype(BF16)
    rk_ref[...] = rope(p0[:, _O_RK:_O_RV] * (RET_QK ** -0.5)).astype(BF16)
    p2 = proj(_O_WQ, _O_CQ)
    rv_ref[...] = p1[:, :_O_RG - _O_RV].astype(BF16)
    rg_ref[...] = p1[:, _O_RG - _O_RV:].astype(BF16)
    p3 = proj(_O_CQ, _O_KR)
    pkr = jnp.dot(a_ref[...], wkr_ref[...], preferred_element_type=F32)
    wq_ref[...] = (p2[:, :_O_WK - _O_WQ] * (SWA_HD ** -0.5)).astype(BF16)
    wk_ref[...] = jnp.dot(p2[:, _O_WK - _O_WQ:_O_WV - _O_WQ].astype(BF16), dup,
                          preferred_element_type=F32).astype(BF16)
    wv_ref[...] = jnp.dot(p2[:, _O_WV - _O_WQ:].astype(BF16), dup, preferred_element_type=F32).astype(BF16)
    cq_ref[...] = p3[:, :_O_CKV - _O_CQ].astype(BF16)
    ckv_ref[...] = p3[:, _O_CKV - _O_CQ:].astype(BF16)
    kr2 = jnp.dot(pkr.astype(BF16), krsel_ref[...], preferred_element_type=F32)
    kr = kr2[:, :128] * cos[:, :128] + kr2[:, 128:] * sin[:, :128]
    kr_ref[:, 0:128] = jnp.zeros_like(kr).astype(BF16)
    kr_ref[:, 128:256] = kr.astype(BF16)


def _lane_maps():
    half = RET_QK // 2
    j = np.arange(RET_HEADS * RET_QK)
    swap = np.zeros((j.size, j.size), np.float32)
    swap[(j // RET_QK) * RET_QK + (j % RET_QK + half) % RET_QK, j] = 1.0
    i = np.arange(MLA_ROPE)
    krsel = np.zeros((MLA_ROPE, 256), np.float32)
    krsel[i, i] = 1.0
    krsel[(i + half) % MLA_ROPE, 128 + i] = 1.0
    c = np.arange(256)
    dup = np.zeros((128, 256), np.float32)
    dup[(c // 128) * SWA_HD + c % SWA_HD, c] = 1.0
    return jnp.asarray(swap, BF16), jnp.asarray(krsel, BF16), jnp.asarray(dup, BF16)


def _in_projection(h2, mod, gains, w_in_b, w_kr_b, cos4, sin4, seq, layer):
    tokens, d = h2.shape
    n_in = w_in_b.shape[-1]
    tm = 512
    per_seq = seq // tm
    widths = (512, 256, 256, 512, 512, 512, 256, 256, 384, 128, 256)
    dtypes = (F32,) + (BF16,) * 10
    row_blk = lambda w: pl.BlockSpec((tm, w), lambda i: (i, 0))
    const = lambda a: pl.BlockSpec(a.shape, lambda i: (0, 0))
    swap, krsel, dup = _lane_maps()
    return pl.pallas_call(
        functools.partial(_inproj_kernel, per_seq=per_seq),
        out_shape=tuple(jax.ShapeDtypeStruct((tokens, w), t) for w, t in zip(widths, dtypes)),
        grid=(tokens // tm,),
        in_specs=[
            row_blk(d),
            _mod_spec(layer, MOD_SC1, 1),
            _mod_spec(layer, MOD_SH1, 1),
            _layer_vec(d, layer, 1),
            _resident((None, d, n_in), lambda i: (layer, 0, 0)),
            pl.BlockSpec((None, d, MLA_ROPE), lambda i: (layer, 0, 0)),
            pl.BlockSpec((tm, 256), lambda i: (i % per_seq, 0)),
            pl.BlockSpec((tm, 256), lambda i: (i % per_seq, 0)),
            const(swap), const(krsel), const(dup),
        ],
        out_specs=tuple(row_blk(w) for w in widths),
        scratch_shapes=[pltpu.VMEM((tm, d), BF16)],
        compiler_params=_cparams(("parallel",), 56),
        name="in_projection",
    )(h2, mod, mod, gains, w_in_b, w_kr_b, cos4, sin4, swap, krsel, dup)


def _s5_prep_kernel(lr_ref, li_ref, ldt_ref, btr_ref, bti_ref, cr_ref, ci_ref, dlag0_ref,
                    a_ref, mint_ref, moutt_ref, pa_ref, pb_ref, *, chunk, nlev):
    width = chunk * S5_CH
    lr = lr_ref[...]
    li = li_ref[...]
    dt = jnp.exp(ldt_ref[...])
    btr, bti = btr_ref[...], bti_ref[...]
    cre, cim = cr_ref[...], ci_ref[...]

    def cpow(k):
        mag = jnp.exp(lr * dt * k)
        ang = li * dt * k
        return mag * jnp.cos(ang), mag * jnp.sin(ang)

    one = jnp.ones((1, 1), F32)
    ar, ai = cpow(one)
    den = lr * lr + li * li
    kr = ((ar - 1.0) * lr + ai * li) / den
    ki = (ai * lr - (ar - 1.0) * li) / den
    bbr = kr * btr - ki * bti
    bbi = kr * bti + ki * btr

    kidx = lax.broadcasted_iota(jnp.int32, (chunk, 1), 0).astype(F32)

    def outer(pr, pi, mr, mi):
        re = pr[:, None, :] * mr[None, :, :] - pi[:, None, :] * mi[None, :, :]
        im = pr[:, None, :] * mi[None, :, :] + pi[:, None, :] * mr[None, :, :]
        return re.reshape(width, S5_STATE), im.reshape(width, S5_STATE)

    er, ei = cpow((chunk - 1.0) - kidx)
    m_r, m_i = outer(er, ei, bbr, bbi)
    m_in = jnp.concatenate([m_r, m_i], axis=1)
    mint_ref[...] = m_in.T.astype(BF16)

    fr, fi = cpow(kidx + 1.0)
    w_r, w_i = outer(fr, fi, cre, cim)
    moutt_ref[...] = jnp.concatenate([w_r, -w_i], axis=1).astype(BF16)

    cc = jnp.concatenate([cre, -cim], axis=1)
    ks = lax.dot_general(cc, m_in, (((1,), (1,)), ((), ())), precision=lax.Precision.HIGHEST,
                         preferred_element_type=F32)
    ks = ks + dlag0_ref[...]
    lane = lax.broadcasted_iota(jnp.int32, (S5_CH, width), 1)
    for t in range(chunk):
        if t == chunk - 1:
            rows = ks
        else:
            shift = (width - S5_CH * (chunk - 1 - t)) % width
            rows = jnp.where(lane < S5_CH * (t + 1), pltpu.roll(ks, shift, axis=1), 0.0)
        a_ref[S5_CH * t:S5_CH * (t + 1), :] = rows.astype(BF16)

    lev = lax.broadcasted_iota(jnp.int32, (8, 1), 0)
    pr, pi = cpow((chunk * jnp.left_shift(1, jnp.minimum(lev, nlev))).astype(F32))
    reps = (2 * S5_STATE) // 8
    pa_ref[...] = jnp.tile(jnp.concatenate([pr, pr], axis=1), (reps, 1)).T
    pb_ref[...] = jnp.tile(jnp.concatenate([-pi, pi], axis=1), (reps, 1)).T


def _s5_prepare(lam_re, lam_im, log_dt, b_re, b_im, c_re, c_im, d_skip, chunk, nlev):
    p = lam_re.shape[-1]
    h = S5_CH
    g = lam_re.shape[0] * lam_re.shape[1]
    width = chunk * h
    d_lag0 = jnp.concatenate([jnp.zeros((g, h, width - h), F32),
                              d_skip.reshape(g, h, 1) * jnp.eye(h, dtype=F32)], axis=-1)
    p2 = 2 * p
    vec = lambda: pl.BlockSpec((None, 1, p), lambda i: (i, 0, 0))
    mat = lambda: pl.BlockSpec((None, h, p), lambda i: (i, 0, 0))
    out = lambda r, c: pl.BlockSpec((None, r, c), lambda i: (i, 0, 0))
    return pl.pallas_call(
        functools.partial(_s5_prep_kernel, chunk=chunk, nlev=nlev),
        out_shape=(
            jax.ShapeDtypeStruct((g, width, width), BF16),
            jax.ShapeDtypeStruct((g, p2, width), BF16),
            jax.ShapeDtypeStruct((g, width, p2), BF16),
            jax.ShapeDtypeStruct((g, p2, p2), F32),
            jax.ShapeDtypeStruct((g, p2, p2), F32),
        ),
        grid=(g,),
        in_specs=[vec(), vec(), vec(), mat(), mat(), mat(), mat(),
                  pl.BlockSpec((None, h, width), lambda i: (i, 0, 0))],
        out_specs=(out(width, width), out(p2, width), out(width, p2), out(p2, p2), out(p2, p2)),
        compiler_params=_cparams(("parallel",), 32),
        name="s5_prepare",
    )(lam_re.reshape(g, 1, p), lam_im.reshape(g, 1, p),
      jnp.broadcast_to(log_dt.reshape(g, 1, 1), (g, 1, p)),
      jnp.swapaxes(b_re, -1, -2).reshape(g, h, p), jnp.swapaxes(b_im, -1, -2).reshape(g, h, p),
      c_re.reshape(g, h, p), c_im.reshape(g, h, p), d_lag0)


def _s5_kernel(u_ref, a_ref, mint_ref, moutt_ref, pa_ref, pb_ref, eye_c_ref, eye_r_ref, y_ref, ut_ref, yt_ref,
               *, chunk, ncb, nlev):
    gps = S5_GROUPS_PER_STEP
    nchunks = ut_ref.shape[-1]
    nt = (((1,), (1,)), ((), ()))
    eye_c, eye_r = eye_c_ref[...], eye_r_ref[...]
    for s in range(chunk):
        slab = u_ref[pl.ds(s, nchunks, stride=chunk), :].astype(BF16)
        slab_t = lax.dot_general(eye_c, slab, nt, preferred_element_type=F32)
        for g in range(gps):
            ut_ref[g, S5_CH * s:S5_CH * (s + 1), :] = slab_t[S5_CH * g:S5_CH * (g + 1), :].astype(BF16)

    cidx = lax.broadcasted_iota(jnp.int32, (1, nchunks), 1) % ncb
    xs = []
    for g in range(gps):
        ut = ut_ref[g]
        yt_ref[g] = jnp.dot(a_ref[g], ut, preferred_element_type=F32)
        xs.append(jnp.dot(mint_ref[g], ut, preferred_element_type=F32))
    for k in range(nlev):
        d = 1 << k
        for g in range(gps):
            sh = jnp.where(cidx >= d, pltpu.roll(xs[g], d, axis=1), 0.0)
            shs = jnp.concatenate([sh[S5_STATE:], sh[:S5_STATE]], axis=0)
            xs[g] = xs[g] + pa_ref[g, :, k:k + 1] * sh + pb_ref[g, :, k:k + 1] * shs
    for g in range(gps):
        xp = jnp.where(cidx >= 1, pltpu.roll(xs[g], 1, axis=1), 0.0)
        yt_ref[g] += jnp.dot(moutt_ref[g], xp.astype(BF16), preferred_element_type=F32)

    for s in range(chunk):
        z = jnp.concatenate([yt_ref[g, S5_CH * s:S5_CH * (s + 1), :] for g in range(gps)], axis=0)
        z_hi = z.astype(BF16)
        z_lo = (z - z_hi.astype(F32)).astype(BF16)
        y_ref[pl.ds(s, nchunks, stride=chunk), :] = (
            lax.dot_general(eye_r, z_hi, nt, preferred_element_type=F32)
            + lax.dot_general(eye_r, z_lo, nt, preferred_element_type=F32))


def _s5_mixer_pre_glu(u, ops, bsz, seq, layer):
    a_op, mint, moutt, pa, pb = ops
    chunk = S5_CHUNK
    ncb = seq // chunk
    nlev = max(1, (ncb - 1).bit_length())
    nb = min(S5_BATCH_PER_STEP, bsz)
    gps = S5_GROUPS_PER_STEP
    nsg = S5_GROUPS // gps
    width = chunk * S5_CH
    p2 = 2 * S5_STATE
    nchunks = nb * ncb
    grp = lambda r, c: pl.BlockSpec((gps, r, c), lambda sg, bp: (layer * nsg + sg, 0, 0))
    tok = pl.BlockSpec((nb * seq, LANES), lambda sg, bp: (bp, sg))
    return pl.pallas_call(
        functools.partial(_s5_kernel, chunk=chunk, ncb=ncb, nlev=nlev),
        out_shape=jax.ShapeDtypeStruct((bsz * seq, GROUP_WIDTH), F32),
        grid=(nsg, bsz // nb),
        in_specs=[tok, grp(width, width), grp(p2, width), grp(width, p2), grp(p2, p2), grp(p2, p2),
                  pl.BlockSpec((LANES, LANES), lambda sg, bp: (0, 0)),
                  pl.BlockSpec((nchunks, nchunks), lambda sg, bp: (0, 0))],
        out_specs=tok,
        scratch_shapes=[pltpu.VMEM((gps, width, nchunks), BF16), pltpu.VMEM((gps, width, nchunks), F32)],
        compiler_params=_cparams(("parallel", "parallel"), 48),
        name="s5_mixer",
    )(u, a_op, mint, moutt, pa, pb, jnp.eye(LANES, dtype=BF16), jnp.eye(nchunks, dtype=BF16))


def _ret_kernel(q_ref, k_ref, v_ref, g_ref, dec_ref, xi_ref, zeta_ref, gc_ref, bd_ref, o_ref, st_ref):
    @pl.when(pl.program_id(1) == 0)
    def _():
        st_ref[...] = jnp.zeros_like(st_ref)

    q = q_ref[...]
    k = k_ref[...]
    v = v_ref[...]
    lane_head = lax.broadcasted_iota(jnp.int32, (1, RET_HEADS * RET_QK), 1) // RET_QK
    st = st_ref[...]
    o_cross = jnp.dot(q, st.astype(BF16), preferred_element_type=F32) * xi_ref[...]
    scores = []
    for hd in range(RET_HEADS):
        km = jnp.where(lane_head == hd, k, jnp.zeros_like(k))
        scores.append(lax.dot_general(q, km, (((1,), (1,)), ((), ())), preferred_element_type=F32))
    probs = [(scores[hd] * dec_ref[hd]).astype(BF16) for hd in range(RET_HEADS)]
    outs = []
    for hd in range(RET_HEADS):
        vh = v[:, hd * RET_V:(hd + 1) * RET_V]
        o = jnp.dot(probs[hd], vh, preferred_element_type=F32) + o_cross[:, hd * RET_V:(hd + 1) * RET_V]
        outs.append(o * lax.rsqrt(jnp.mean(o * o, axis=-1, keepdims=True) + EPS))
    on = jnp.concatenate(outs, axis=1)
    gate = g_ref[...].astype(F32)
    o_ref[...] = (on * (gate * jax.nn.sigmoid(gate))).astype(BF16)

    kz = (k.astype(F32) * zeta_ref[...]).astype(BF16)
    kv = lax.dot_general(kz, v, (((0,), (0,)), ((), ())), preferred_element_type=F32)
    st_ref[...] = st * gc_ref[...] + kv * bd_ref[...]


def _retention(rq, rk, rv, rg, bsz, seq):
    cb = min(RET_BLOCK, seq)
    nblk = seq // cb
    hq = RET_HEADS * RET_QK
    log_gamma = jnp.log1p(-(2.0 ** (-5.0 - jnp.arange(RET_HEADS, dtype=F32))))
    idx = jnp.arange(cb, dtype=F32)
    rel = idx[:, None] - idx[None, :]
    decay = jnp.where(rel >= 0, jnp.exp(log_gamma[:, None, None] * jnp.maximum(rel, 0.0)), 0.0)
    xi = jnp.repeat(jnp.exp(log_gamma[None, :] * (idx[:, None] + 1.0)), RET_V, axis=1)
    zeta = jnp.repeat(jnp.exp(log_gamma[None, :] * (cb - 1.0 - idx[:, None])), RET_QK, axis=1)
    gchunk = jnp.repeat(jnp.exp(log_gamma * cb), RET_V)[None, :]
    blockdiag = (jnp.arange(hq)[:, None] // RET_QK == jnp.arange(GROUP_WIDTH)[None, :] // RET_V).astype(F32)
    tok = lambda w: pl.BlockSpec((cb, w), lambda b, c: (b * nblk + c, 0))
    const = lambda shape: pl.BlockSpec(shape, lambda b, c: (0,) * len(shape))
    return pl.pallas_call(
        _ret_kernel,
        out_shape=jax.ShapeDtypeStruct((bsz * seq, GROUP_WIDTH), BF16),
        grid=(bsz, nblk),
        in_specs=[tok(hq), tok(hq), tok(GROUP_WIDTH), tok(GROUP_WIDTH),
                  const((RET_HEADS, cb, cb)), const((cb, GROUP_WIDTH)), const((cb, hq)),
                  const((1, GROUP_WIDTH)), const((hq, GROUP_WIDTH))],
        out_specs=tok(GROUP_WIDTH),
        scratch_shapes=[pltpu.VMEM((hq, GROUP_WIDTH), F32)],
        compiler_params=_cparams(("parallel", "arbitrary"), 32),
        name="retention",
    )(rq, rk, rv, rg, decay, xi, zeta, gchunk, blockdiag)


def _swa_kernel(sink_ref, q_ref, kc_ref, kp_ref, vc_ref, vp_ref, o_ref, *, nsub):
    blk = pl.program_id(1)
    kall = jnp.concatenate([kp_ref[...], kc_ref[...]], axis=0)
    vall = jnp.concatenate([vp_ref[...], vc_ref[...]], axis=0)
    half = lax.broadcasted_iota(jnp.int32, (1, 2 * SWA_HD), 1) // SWA_HD
    r = lax.broadcasted_iota(jnp.int32, (WINDOW, 2 * WINDOW), 0)
    jc = lax.broadcasted_iota(jnp.int32, (WINDOW, 2 * WINDOW), 1)
    dist = r + WINDOW - jc
    band = jnp.logical_and(dist >= 0, dist < WINDOW)
    first_key = jnp.where(blk > 0, 0, WINDOW)
    band_first = jnp.logical_and(band, jc >= first_key)
    items = [(sb, j) for sb in range(nsub) for j in range(SWA_KV_HEADS)]
    scores, probs, invs_all = [], [], []
    for sb, j in items:
        r0 = sb * WINDOW
        kk = kall[r0:r0 + 2 * WINDOW, 128 * j:128 * (j + 1)]
        zk = jnp.zeros_like(kk)
        kcat = jnp.concatenate([jnp.where(half == 0, kk, zk), jnp.where(half == 1, kk, zk)], axis=0)
        q2 = jnp.concatenate([q_ref[r0:r0 + WINDOW, 256 * j:256 * j + 128],
                              q_ref[r0:r0 + WINDOW, 256 * j + 128:256 * j + 256]], axis=0)
        scores.append(lax.dot_general(q2, kcat, (((1,), (1,)), ((), ())), preferred_element_type=F32))
    for (sb, j), s_all in zip(items, scores):
        valid = band_first if sb == 0 else band
        p_rows, inv_rows = [], []
        for a in range(2):
            p_cols, invs = [], []
            for e in range(2):
                sink = sink_ref[4 * j + 2 * a + e]
                s = s_all[WINDOW * a:WINDOW * (a + 1), 2 * WINDOW * e:2 * WINDOW * (e + 1)]
                s = jnp.where(valid, s, NEG)
                m = jnp.maximum(jnp.max(s, axis=-1, keepdims=True), sink)
                p = jnp.exp(s - m)
                denom = jnp.sum(p, axis=-1, keepdims=True) + jnp.exp(sink - m)
                p_cols.append(p.astype(BF16))
                invs.append(1.0 / denom)
            p_rows.append(jnp.concatenate(p_cols, axis=1))
            inv_rows.append(jnp.where(half == 0, invs[0], invs[1]))
        probs.append(jnp.concatenate(p_rows, axis=0))
        invs_all.append(inv_rows)
    for (sb, j), p_all, inv_rows in zip(items, probs, invs_all):
        r0 = sb * WINDOW
        vv = vall[r0:r0 + 2 * WINDOW, 128 * j:128 * (j + 1)]
        zv = jnp.zeros_like(vv)
        vcat = jnp.concatenate([jnp.where(half == 0, vv, zv), jnp.where(half == 1, vv, zv)], axis=0)
        o = jnp.dot(p_all, vcat, preferred_element_type=F32)
        for a in range(2):
            o_ref[r0:r0 + WINDOW, 256 * j + 128 * a:256 * j + 128 * (a + 1)] = (
                o[WINDOW * a:WINDOW * (a + 1)] * inv_rows[a]).astype(BF16)


def _swa(wq, wk2, wv2, sinks, bsz, seq, layer):
    qb = min(SWA_BLOCK, seq)
    nblk = seq // qb
    per = qb // WINDOW
    cur = lambda w: pl.BlockSpec((qb, w), lambda b, i: (b * nblk + i, 0))
    prev = lambda w: pl.BlockSpec((WINDOW, w), lambda b, i: (jnp.maximum((b * nblk + i) * per - 1, 0), 0))
    return pl.pallas_call(
        functools.partial(_swa_kernel, nsub=per),
        out_shape=jax.ShapeDtypeStruct((bsz * seq, GROUP_WIDTH), BF16),
        grid=(bsz, nblk),
        in_specs=[pl.BlockSpec(memory_space=pltpu.SMEM),
                  cur(GROUP_WIDTH), cur(256), prev(256), cur(256), prev(256)],
        out_specs=cur(GROUP_WIDTH),
        compiler_params=_cparams(("parallel", "parallel"), 32),
        name="swa",
    )(sinks[layer], wq, wk2, wk2, wv2, wv2)


def _mla_up_kernel(cq_ref, ckv_ref, kr_ref, qn_ref, kvn_ref, wq_ref, wkv_ref, cos_ref, sin_ref,
                   q_ref, k_ref, vt_ref):
    cq = cq_ref[...].astype(F32)
    nq = (cq * lax.rsqrt(jnp.mean(cq * cq, axis=-1, keepdims=True) + EPS) * qn_ref[...]).astype(BF16)
    qq = jnp.dot(nq, wq_ref[...], preferred_element_type=F32)
    ckv = ckv_ref[...].astype(F32)
    nkv = (ckv * lax.rsqrt(jnp.mean(ckv * ckv, axis=-1, keepdims=True) + EPS) * kvn_ref[...]).astype(BF16)
    kv = jnp.dot(nkv, wkv_ref[...], preferred_element_type=F32)
    cos, sin = cos_ref[...], sin_ref[...]
    kr = kr_ref[...].astype(F32)
    hw = MLA_QK_PAD
    for hd in range(MLA_HEADS):
        q_ref[:, hw * hd:hw * (hd + 1)] = (
            qq[:, hw * hd:hw * (hd + 1)] * cos
            + qq[:, hw * (MLA_HEADS + hd):hw * (MLA_HEADS + hd + 1)] * sin).astype(BF16)
        k_ref[:, hw * hd:hw * (hd + 1)] = (kv[:, hw * hd:hw * (hd + 1)] + kr).astype(BF16)
    vt_ref[...] = kv[:, hw * MLA_HEADS:].T.astype(BF16)


def _mla_up(cq, ckv, kr_pad, q_norm, kv_norm, wq_ext, wkv_ext, cosq, sinq, seq, layer):
    tokens = cq.shape[0]
    tm = min(MLA_TK, seq)
    per_seq = seq // tm
    hw = MLA_QK_PAD
    row = lambda w: pl.BlockSpec((tm, w), lambda i: (i, 0))
    tab = lambda: pl.BlockSpec((tm, hw), lambda i: (i % per_seq, 0))
    lay = lambda a: pl.BlockSpec((None,) + a.shape[1:], lambda i: (layer, 0, 0))
    return pl.pallas_call(
        _mla_up_kernel,
        out_shape=(jax.ShapeDtypeStruct((tokens, MLA_HEADS * hw), BF16),
                   jax.ShapeDtypeStruct((tokens, MLA_HEADS * hw), BF16),
                   jax.ShapeDtypeStruct((tokens // tm, MLA_HEADS * MLA_V, tm), BF16)),
        grid=(tokens // tm,),
        in_specs=[row(MLA_Q_RANK), row(MLA_KV_RANK), row(hw), lay(q_norm), lay(kv_norm),
                  lay(wq_ext), lay(wkv_ext), tab(), tab()],
        out_specs=(row(MLA_HEADS * hw), row(MLA_HEADS * hw),
                   pl.BlockSpec((None, MLA_HEADS * MLA_V, tm), lambda i: (i, 0, 0))),
        compiler_params=_cparams(("parallel",), 40),
        name="mla_up",
    )(cq, ckv, kr_pad, q_norm, kv_norm, wq_ext, wkv_ext, cosq, sinq)


def _mla_attn_kernel(q_ref, k_ref, vt_ref, o_ref, m_ref, l_ref, acc_ref, *, tq, tk, heads):
    qi = pl.program_id(2)
    hw = MLA_QK_PAD
    m_ref[...] = jnp.full_like(m_ref, NEG)
    l_ref[...] = jnp.zeros_like(l_ref)
    acc_ref[...] = jnp.zeros_like(acc_ref)

    def step(kbs, masked):
        sts, ps, alphas = [], [], []
        for hd in range(heads):
            q = q_ref[:, hw * hd:hw * (hd + 1)]
            blocks = []
            for kb in kbs:
                ks = k_ref[pl.ds(pl.multiple_of(kb * tk, tk), tk), hw * hd:hw * (hd + 1)]
                st = lax.dot_general(ks, q, (((1,), (1,)), ((), ())), preferred_element_type=F32)
                if masked:
                    key = lax.broadcasted_iota(jnp.int32, (tk, tq), 0)
                    qry = lax.broadcasted_iota(jnp.int32, (tk, tq), 1)
                    st = jnp.where(key <= qry, st, NEG)
                blocks.append(st)
            sts.append(blocks)
        for hd in range(heads):
            m_old = m_ref[hd]
            m_new = m_old
            for st in sts[hd]:
                m_new = jnp.maximum(m_new, jnp.max(st, axis=0, keepdims=True))
            alpha = jnp.exp2(m_old - m_new)
            l_new = alpha * l_ref[hd]
            blocks = []
            for st in sts[hd]:
                p = jnp.exp2(st - m_new)
                l_new = l_new + jnp.sum(p, axis=0, keepdims=True)
                blocks.append(p.astype(BF16))
            l_ref[hd] = l_new
            m_ref[hd] = m_new
            ps.append(blocks)
            alphas.append(alpha)
        for hd in range(heads):
            acc = alphas[hd] * acc_ref[hd]
            for kb, p in zip(kbs, ps[hd]):
                acc = acc + jnp.dot(vt_ref[kb, MLA_V * hd:MLA_V * (hd + 1), :], p, preferred_element_type=F32)
            acc_ref[hd] = acc

    def body(pair, carry):
        step((2 * pair, 2 * pair + 1), False)
        return carry

    lax.fori_loop(0, qi // 2, body, 0)

    @pl.when(qi % 2 == 1)
    def _():
        step((qi - 1,), False)

    step((qi,), True)
    for hd in range(heads):
        o_ref[:, MLA_V * hd:MLA_V * (hd + 1)] = (acc_ref[hd] / l_ref[hd]).T.astype(BF16)


def _mla_attention(q, k, vt, bsz, seq):
    tq = tk = min(MLA_TQ, seq)
    nq = seq // tq
    hp = MLA_HEADS_PER_STEP
    hw = MLA_QK_PAD
    return pl.pallas_call(
        functools.partial(_mla_attn_kernel, tq=tq, tk=tk, heads=hp),
        out_shape=jax.ShapeDtypeStruct((bsz * seq, MLA_HEADS * MLA_V), BF16),
        grid=(bsz, MLA_HEADS // hp, nq),
        in_specs=[pl.BlockSpec((tq, hp * hw), lambda b, h, i: (b * nq + i, h)),
                  pl.BlockSpec((seq, hp * hw), lambda b, h, i: (b, h)),
                  pl.BlockSpec((nq, hp * MLA_V, tk), lambda b, h, i: (b, h, 0))],
        out_specs=pl.BlockSpec((tq, hp * MLA_V), lambda b, h, i: (b * nq + i, h)),
        scratch_shapes=[pltpu.VMEM((hp, 1, tq), F32), pltpu.VMEM((hp, 1, tq), F32),
                        pltpu.VMEM((hp, MLA_V, tq), F32)],
        compiler_params=_cparams(("parallel", "parallel", "arbitrary"), 48),
        name="mla_attention",
    )(q, k, vt)


def _outproj_kernel(ys5_ref, yret_ref, yswa_ref, ymla_ref, h_ref, gt_ref, gw_ref, gb_ref, w_ref, o_ref,
                    *, per_seq):
    b = pl.program_id(0) // per_seq
    y = ys5_ref[...]
    z = jax.nn.gelu(y, approximate=True)
    gl = jnp.dot(z.astype(BF16), gw_ref[...], preferred_element_type=F32) + gb_ref[...]
    s5 = (z * jax.nn.sigmoid(gl)).astype(BF16)
    gw = GROUP_WIDTH
    mixed = jnp.dot(s5, w_ref[0:gw, :], preferred_element_type=F32)
    mixed += jnp.dot(yret_ref[...], w_ref[gw:2 * gw, :], preferred_element_type=F32)
    mixed += jnp.dot(yswa_ref[...], w_ref[2 * gw:3 * gw, :], preferred_element_type=F32)
    mixed += jnp.dot(ymla_ref[...], w_ref[3 * gw:4 * gw, :], preferred_element_type=F32)
    o_ref[...] = h_ref[...] + gt_ref[pl.ds(b, 1), :] * mixed


def _out_projection(ys5, yret, yswa, ymla, h2, mod, glu_w, glu_b, w_out, seq, layer):
    tokens, d = h2.shape
    tm = min(512, seq)
    per_seq = seq // tm
    gw = GROUP_WIDTH
    row = lambda w: pl.BlockSpec((tm, w), lambda i: (i, 0))
    return pl.pallas_call(
        functools.partial(_outproj_kernel, per_seq=per_seq),
        out_shape=jax.ShapeDtypeStruct((tokens, d), F32),
        grid=(tokens // tm,),
        in_specs=[row(gw), row(gw), row(gw), row(gw), row(d),
                  _mod_spec(layer, MOD_GT1, 1),
                  pl.BlockSpec((None, gw, gw), lambda i: (layer, 0, 0)),
                  _layer_vec(gw, layer, 1),
                  _resident((None, 4 * gw, d), lambda i: (layer, 0, 0))],
        out_specs=row(d),
        compiler_params=_cparams(("parallel",), 48),
        name="out_projection",
    )(ys5, yret, yswa, ymla, h2, mod, glu_w, glu_b, w_out)


def _mlp_kernel(h_ref, sc_ref, sh_ref, gt_ref, g_ref, fg_ref, w1_ref, w2_ref, o_ref, a_ref, *, per_seq, final):
    j = pl.program_id(1)
    b = pl.program_id(0) // per_seq

    @pl.when(j == 0)
    def _():
        _norm_modulate(h_ref, g_ref[...] * (1.0 + sc_ref[pl.ds(b, 1), :]), sh_ref[pl.ds(b, 1), :], a_ref)
        o_ref[...] = jnp.zeros_like(o_ref)

    hid = jnp.dot(a_ref[...], w1_ref[...], preferred_element_type=F32)
    hid = jnp.square(jnp.maximum(hid, 0.0)).astype(BF16)
    o_ref[...] += jnp.dot(hid, w2_ref[...], preferred_element_type=F32)

    @pl.when(j == pl.num_programs(1) - 1)
    def _():
        gate = gt_ref[pl.ds(b, 1), :]
        for r in range(0, o_ref.shape[0], NORM_ROWS):
            out = h_ref[r:r + NORM_ROWS, :] + gate * o_ref[r:r + NORM_ROWS, :]
            if final:
                ms = jnp.mean(out * out, axis=-1, keepdims=True)
                out = out * lax.rsqrt(ms + EPS) * fg_ref[...]
            o_ref[r:r + NORM_ROWS, :] = out


def _mlp(h2, mod, gains, final_gain, w1, w2, seq, layer, final):
    tokens, d = h2.shape
    dff = w1.shape[-1]
    tm = min(512, seq)
    tf = 1024
    per_seq = seq // tm
    return pl.pallas_call(
        functools.partial(_mlp_kernel, per_seq=per_seq, final=final),
        out_shape=jax.ShapeDtypeStruct((tokens, d), F32),
        grid=(tokens // tm, dff // tf),
        in_specs=[pl.BlockSpec((tm, d), lambda i, j: (i, 0)),
                  _mod_spec(layer, MOD_SC2, 2), _mod_spec(layer, MOD_SH2, 2), _mod_spec(layer, MOD_GT2, 2),
                  _layer_vec(d, layer, 2),
                  pl.BlockSpec((1, d), lambda i, j: (0, 0)),
                  pl.BlockSpec((None, d, tf), lambda i, j: (layer, 0, j)),
                  pl.BlockSpec((None, tf, d), lambda i, j: (layer, j, 0))],
        out_specs=pl.BlockSpec((tm, d), lambda i, j: (i, 0)),
        scratch_shapes=[pltpu.VMEM((tm, d), BF16)],
        compiler_params=_cparams(("parallel", "arbitrary"), 48),
        name="mlp",
    )(h2, mod, mod, mod, gains, final_gain, w1, w2)


def _swap_halves(w, heads, hd):
    lead = w.shape[:-1]
    w4 = w.reshape(lead + (heads, 2, hd // 2))
    return jnp.concatenate([w4[..., 1:, :], w4[..., :1, :]], axis=-2).reshape(lead + (heads * hd,))


def _mla_weight_ext(w_uq, w_ukv):
    hq = MLA_NOPE + MLA_ROPE
    hk = MLA_NOPE + MLA_V
    scale = hq ** -0.5 * math.log2(math.e)
    zq = lambda n: jnp.zeros(w_uq.shape[:-1] + (n,), w_uq.dtype)
    zk128 = jnp.zeros(w_ukv.shape[:-1] + (128,), w_ukv.dtype)
    plain, swapped, kcols, vcols = [], [], [], []
    for h in range(MLA_HEADS):
        plain += [w_uq[..., h * hq:(h + 1) * hq], zq(64)]
        swapped += [zq(128), _swap_halves(w_uq[..., h * hq + MLA_NOPE:(h + 1) * hq], 1, MLA_ROPE), zq(64)]
        kcols += [w_ukv[..., h * hk:h * hk + MLA_NOPE], zk128]
        vcols += [w_ukv[..., h * hk + MLA_NOPE:(h + 1) * hk]]
    wq_ext = (jnp.concatenate(plain + swapped, axis=-1) * scale).astype(BF16)
    wkv_ext = jnp.concatenate(kcols + vcols, axis=-1).astype(BF16)
    return wq_ext, wkv_ext


def _rotary_tables(seq):
    d = RET_QK
    inv = ROPE_BASE ** (-jnp.arange(0, d, 2, dtype=F32) / d)
    ang = jnp.arange(seq, dtype=F32)[:, None] * inv[None, :]
    cos, sin = jnp.cos(ang), jnp.sin(ang)
    cos1 = jnp.concatenate([cos, cos], axis=1)
    sin1 = jnp.concatenate([-sin, sin], axis=1)
    cos4, sin4 = jnp.tile(cos1, (1, 4)), jnp.tile(sin1, (1, 4))
    zeros64 = jnp.zeros((seq, 64), F32)
    cosq = jnp.concatenate([jnp.ones((seq, MLA_NOPE), F32), cos1, zeros64], axis=1)
    sinq = jnp.concatenate([jnp.zeros((seq, MLA_NOPE), F32), sin1, zeros64], axis=1)
    return cos4, sin4, cosq, sinq


def kernel(x, c, norm1_g, norm2_g, ada_w, ada_b, w_in, s5_lambda_re, s5_lambda_im, s5_log_dt, s5_b_re, s5_b_im, s5_c_re, s5_c_im, s5_d, s5_glu_w, s5_glu_b, swa_sinks, mla_q_norm, mla_kv_norm, mla_w_uq, mla_w_ukv, w_out, mlp_w1, mlp_w2, final_norm_g):
    bsz, seq, d = x.shape
    depth = ada_w.shape[0]
    tokens = bsz * seq
    h = x.reshape(tokens, d)
    mod = _adaln_mod(c, ada_w, ada_b)
    cos4, sin4, cosq, sinq = _rotary_tables(seq)
    w_ext = w_in[..., :_O_KR].astype(BF16)
    w_kr = w_in[..., _O_KR:].astype(BF16)
    wq_ext, wkv_ext = _mla_weight_ext(mla_w_uq, mla_w_ukv)
    w_out_b, glu_w_b = w_out.astype(BF16), s5_glu_w.astype(BF16)
    w1_b, w2_b = mlp_w1.astype(BF16), mlp_w2.astype(BF16)
    norm1 = norm1_g.reshape(depth, 1, d)
    norm2 = norm2_g.reshape(depth, 1, d)
    glu_b = s5_glu_b.reshape(depth, 1, -1)
    q_norm = mla_q_norm.reshape(depth, 1, -1)
    kv_norm = mla_kv_norm.reshape(depth, 1, -1)
    ncb = seq // S5_CHUNK
    s5_ops = _s5_prepare(s5_lambda_re, s5_lambda_im, s5_log_dt, s5_b_re, s5_b_im, s5_c_re, s5_c_im, s5_d,
                         S5_CHUNK, max(1, (ncb - 1).bit_length()))
    for l in range(depth):
        (u, rq, rk, rv, rg, wq, wk2, wv2, cq, ckv, kr_pad) = _in_projection(
            h, mod, norm1, w_ext, w_kr, cos4, sin4, seq, l)
        y_s5 = _s5_mixer_pre_glu(u, s5_ops, bsz, seq, l)
        y_ret = _retention(rq, rk, rv, rg, bsz, seq)
        y_swa = _swa(wq, wk2, wv2, swa_sinks, bsz, seq, l)
        mq, mk, mvt = _mla_up(cq, ckv, kr_pad, q_norm, kv_norm, wq_ext, wkv_ext, cosq, sinq, seq, l)
        y_mla = _mla_attention(mq, mk, mvt, bsz, seq)
        h = _out_projection(y_s5, y_ret, y_swa, y_mla, h, mod, glu_w_b, glu_b, w_out_b, seq, l)
        h = _mlp(h, mod, norm2, final_norm_g.reshape(1, d), w1_b, w2_b, seq, l, final=(l == depth - 1))
    return h.reshape(bsz, seq, d)
```
